```python
import jax, jax.numpy as jnp
from jax import lax
import numpy as np

D_MODEL = 1024
BATCH = 2
SEQ = 8192
DEPTH = 2

W_SC = 1024
SC_KERNEL = 3
W_CF = 1024
CF_KERNEL = 31
W_POOL = 1024
POOL_WINDOWS = (2, 4, 8, 16)
N_POOL_GROUPS = len(POOL_WINDOWS)
POOL_GROUP = W_POOL // N_POOL_GROUPS
N_BRANCH = 3
D_FF = 3584
N_EXPERTS = 8
TOP_K = 2
N_DENSE = (DEPTH + 1) // 2
N_MOE = DEPTH // 2
RMS_EPS = 1e-6
LN_EPS = 1e-5
IN_SPLITS = [W_SC, 2 * W_SC, 3 * W_SC, 3 * W_SC + W_CF, 3 * W_SC + 2 * W_CF,
             3 * W_SC + 2 * W_CF + W_POOL]
IN_COLS = 3 * W_SC + 2 * W_CF + W_POOL + N_BRANCH * D_MODEL

kernel_name = "hybrid_conv_pool_moe_trunk"


def rms_norm(x, g):
    xf = x.astype(jnp.float32)
    y = xf * lax.rsqrt(jnp.mean(xf * xf, axis=-1, keepdims=True) + RMS_EPS)
    return (y * g.astype(jnp.float32)).astype(x.dtype)


def layer_norm(x, g, b):
    xf = x.astype(jnp.float32)
    mu = jnp.mean(xf, axis=-1, keepdims=True)
    xc = xf - mu
    var = jnp.mean(xc * xc, axis=-1, keepdims=True)
    y = xc * lax.rsqrt(var + LN_EPS) * g.astype(jnp.float32) + b.astype(jnp.float32)
    return y.astype(x.dtype)


def causal_depthwise_conv(u, w):
    k, c = w.shape
    return lax.conv_general_dilated(
        u, w[:, None, :].astype(u.dtype), window_strides=(1,), padding=[(k - 1, 0)],
        dimension_numbers=('NWC', 'WIO', 'NWC'), feature_group_count=c)


def multiscale_causal_pool(u):
    uf = u.astype(jnp.float32)
    s = uf.shape[1]
    max_w = max(POOL_WINDOWS)
    cs = jnp.cumsum(uf, axis=1)
    cs_pad = jnp.pad(cs, ((0, 0), (max_w, 0), (0, 0)))
    pos = jnp.arange(s, dtype=jnp.float32)[None, :, None]
    outs = []
    for gi, w in enumerate(POOL_WINDOWS):
        lo, hi = gi * POOL_GROUP, (gi + 1) * POOL_GROUP
        win_sum = cs[:, :, lo:hi] - cs_pad[:, max_w - w:max_w - w + s, lo:hi]
        count = jnp.minimum(pos + 1.0, float(w))
        outs.append(win_sum / count - uf[:, :, lo:hi])
    return jnp.stack(outs, axis=2).astype(u.dtype)


def hybrid_mixer(h, w_in, sc_conv_w, cf_conv_w, cf_conv_b, cf_ln_g, cf_ln_b,
                 pool_w, pool_scale, w_sc_out, w_cf_out, w_pool_out, w_o):
    b, s, _ = h.shape
    proj = jnp.einsum('bsd,dc->bsc', h, w_in)
    sc_b, sc_c, sc_x, cf_v, cf_g, pool_u, gates = jnp.split(proj, IN_SPLITS, axis=-1)

    y_a = sc_b * causal_depthwise_conv(sc_c * sc_x, sc_conv_w)
    y_a = jnp.einsum('bsc,cd->bsd', y_a, w_sc_out)

    v = cf_v * jax.nn.sigmoid(cf_g)
    v = causal_depthwise_conv(v, cf_conv_w) + cf_conv_b.astype(v.dtype)
    v = jax.nn.silu(layer_norm(v, cf_ln_g, cf_ln_b))
    y_b = jnp.einsum('bsc,cd->bsd', v, w_cf_out)

    p = multiscale_causal_pool(pool_u)
    p = jnp.einsum('bsgc,gce->bsge', p, pool_w).reshape(b, s, W_POOL) * pool_scale.astype(p.dtype)
    y_c = jnp.einsum('bsc,cd->bsd', p, w_pool_out)

    g = jax.nn.sigmoid(gates.reshape(b, s, N_BRANCH, D_MODEL))
    merged = g[:, :, 0] * y_a + g[:, :, 1] * y_b + g[:, :, 2] * y_c
    return jnp.einsum('bsd,de->bse', merged, w_o)


def swiglu(h, w1, w3, w2):
    a = jnp.einsum('bsd,df->bsf', h, w1)
    c = jnp.einsum('bsd,df->bsf', h, w3)
    return jnp.einsum('bsf,fd->bsd', jax.nn.silu(a) * c, w2)


def moe_swiglu(h, router_w, w1, w3, w2):
    logits = jnp.einsum('bsd,de->bse', h, router_w).astype(jnp.float32)
    top_v, top_i = lax.top_k(logits, TOP_K)
    probs = jax.nn.softmax(top_v, axis=-1)
    combine = jnp.sum(jax.nn.one_hot(top_i, N_EXPERTS, dtype=jnp.float32) * probs[..., None], axis=-2)
    combine = combine.astype(h.dtype)
    out = jnp.zeros_like(h)
    for e in range(N_EXPERTS):
        out = out + combine[..., e:e + 1] * swiglu(h, w1[e], w3[e], w2[e])
    return out


def setup_inputs(seed: int = 0) -> dict:
    key = jax.random.key(seed)
    ks = jax.random.split(key, 24)

    def nrm(k, shape, scale):
        return jax.random.normal(k, shape, dtype=jnp.float32) * scale

    return {
        "x": nrm(ks[0], (BATCH, SEQ, D_MODEL), 1.0),
        "norm_mix_g": 1.0 + nrm(ks[1], (DEPTH, D_MODEL), 0.02),
        "w_in": nrm(ks[2], (DEPTH, D_MODEL, IN_COLS), D_MODEL ** -0.5),
        "sc_conv_w": nrm(ks[3], (DEPTH, SC_KERNEL, W_SC), SC_KERNEL ** -0.5),
        "cf_conv_w": nrm(ks[4], (DEPTH, CF_KERNEL, W_CF), CF_KERNEL ** -0.5),
        "cf_conv_b": nrm(ks[5], (DEPTH, W_CF), 0.02),
        "cf_ln_g": 1.0 + nrm(ks[6], (DEPTH, W_CF), 0.02),
        "cf_ln_b": nrm(ks[7], (DEPTH, W_CF), 0.02),
        "pool_w": nrm(ks[8], (DEPTH, N_POOL_GROUPS, POOL_GROUP, POOL_GROUP), POOL_GROUP ** -0.5),
        "pool_scale": 1.0 + nrm(ks[9], (DEPTH, W_POOL), 0.02),
        "w_sc_out": nrm(ks[10], (DEPTH, W_SC, D_MODEL), W_SC ** -0.5),
        "w_cf_out": nrm(ks[11], (DEPTH, W_CF, D_MODEL), W_CF ** -0.5),
        "w_pool_out": nrm(ks[12], (DEPTH, W_POOL, D_MODEL), W_POOL ** -0.5),
        "w_o": nrm(ks[13], (DEPTH, D_MODEL, D_MODEL), D_MODEL ** -0.5),
        "norm_ffn_g": 1.0 + nrm(ks[14], (DEPTH, D_MODEL), 0.02),
        "dense_w1": nrm(ks[15], (N_DENSE, D_MODEL, D_FF), D_MODEL ** -0.5),
        "dense_w3": nrm(ks[16], (N_DENSE, D_MODEL, D_FF), D_MODEL ** -0.5),
        "dense_w2": nrm(ks[17], (N_DENSE, D_FF, D_MODEL), D_FF ** -0.5),
        "moe_router": nrm(ks[18], (N_MOE, D_MODEL, N_EXPERTS), D_MODEL ** -0.5),
        "moe_w1": nrm(ks[19], (N_MOE, N_EXPERTS, D_MODEL, D_FF), D_MODEL ** -0.5),
        "moe_w3": nrm(ks[20], (N_MOE, N_EXPERTS, D_MODEL, D_FF), D_MODEL ** -0.5),
        "moe_w2": nrm(ks[21], (N_MOE, N_EXPERTS, D_FF, D_MODEL), D_FF ** -0.5),
        "norm_final_g": 1.0 + nrm(ks[22], (D_MODEL,), 0.02),
    }


def reference(x, norm_mix_g, w_in, sc_conv_w, cf_conv_w, cf_conv_b, cf_ln_g, cf_ln_b,
              pool_w, pool_scale, w_sc_out, w_cf_out, w_pool_out, w_o, norm_ffn_g,
              dense_w1, dense_w3, dense_w2, moe_router, moe_w1, moe_w3, moe_w2,
              norm_final_g):
    for layer in range(DEPTH):
        h = rms_norm(x, norm_mix_g[layer])
        x = x + hybrid_mixer(h, w_in[layer], sc_conv_w[layer], cf_conv_w[layer],
                             cf_conv_b[layer], cf_ln_g[layer], cf_ln_b[layer],
                             pool_w[layer], pool_scale[layer], w_sc_out[layer],
                             w_cf_out[layer], w_pool_out[layer], w_o[layer])
        h = rms_norm(x, norm_ffn_g[layer])
        i = layer // 2
        if layer % 2 == 0:
            x = x + swiglu(h, dense_w1[i], dense_w3[i], dense_w2[i])
        else:
            x = x + moe_swiglu(h, moe_router[i], moe_w1[i], moe_w3[i], moe_w2[i])
    return rms_norm(x, norm_final_g)
```

```python
import functools

import jax
import jax.numpy as jnp
from jax import lax
from jax.experimental import pallas as pl
from jax.experimental.pallas import tpu as pltpu

D_MODEL = 1024
W_BRANCH = 1024
SC_KERNEL = 3
CF_KERNEL = 31
POOL_WINDOWS = (2, 4, 8, 16)
POOL_GROUP = W_BRANCH // len(POOL_WINDOWS)
D_FF = 3584
N_EXPERTS = 8
RMS_EPS = 1e-6
LN_EPS = 1e-5

COL_SC = 0
COL_CF = 3 * W_BRANCH
COL_POOL = 5 * W_BRANCH
COL_GATE = 6 * W_BRANCH
IN_COLS = 9 * W_BRANCH

HALO_SC = 8
HALO_CF = 32
HALO_POOL = 16

MIX_TM = 256
FFN_TM = 512
FFN_FC = 512
VMEM_LIMIT = 60 * 1024 * 1024

F32 = jnp.float32
BF16 = jnp.bfloat16


def _rms_norm(x, g):
    ms = jnp.mean(x * x, axis=-1, keepdims=True)
    return x * lax.rsqrt(ms + RMS_EPS) * g


def _dot(a, b):
    return jnp.dot(a, b, preferred_element_type=F32)


def _mixer_kernel(x_ref, g_ref, w_in_ref, scw_ref, cfw_ref, cfb_ref, lng_ref, lnb_ref,
                  poolw_ref, pscale_ref, wa_ref, wb_ref, wc_ref, wo_ref,
                  out_ref, u_buf, v_buf, p_buf):
    i = pl.program_id(1)
    tm = x_ref.shape[0]

    @pl.when(i == 0)
    def _():
        u_buf[0:HALO_SC, :] = jnp.zeros((HALO_SC, W_BRANCH), F32)
        v_buf[0:HALO_CF, :] = jnp.zeros((HALO_CF, W_BRANCH), F32)
        p_buf[0:HALO_POOL, :] = jnp.zeros((HALO_POOL, W_BRANCH), F32)

    x = x_ref[...]
    h = _rms_norm(x, g_ref[...]).astype(BF16)

    pa = _dot(h, w_in_ref[:, COL_SC:COL_SC + 3 * W_BRANCH])
    u = pa[:, W_BRANCH:2 * W_BRANCH] * pa[:, 2 * W_BRANCH:3 * W_BRANCH]
    u_buf[HALO_SC:HALO_SC + tm, :] = u
    conv = scw_ref[SC_KERNEL - 1:SC_KERNEL, :] * u
    for k in range(SC_KERNEL - 1):
        off = HALO_SC - (SC_KERNEL - 1) + k
        conv = conv + scw_ref[k:k + 1, :] * u_buf[off:off + tm, :]
    y_a = _dot((pa[:, 0:W_BRANCH] * conv).astype(BF16), wa_ref[...])
    u_buf[0:HALO_SC, :] = u_buf[tm:tm + HALO_SC, :]

    pb = _dot(h, w_in_ref[:, COL_CF:COL_CF + 2 * W_BRANCH])
    v = pb[:, 0:W_BRANCH] * jax.nn.sigmoid(pb[:, W_BRANCH:2 * W_BRANCH])
    v_buf[HALO_CF:HALO_CF + tm, :] = v
    acc = cfw_ref[CF_KERNEL - 1:CF_KERNEL, :] * v + cfb_ref[...]
    for k in range(CF_KERNEL - 1):
        off = HALO_CF - (CF_KERNEL - 1) + k
        acc = acc + cfw_ref[k:k + 1, :] * v_buf[off:off + tm, :]
    mu = jnp.mean(acc, axis=-1, keepdims=True)
    xc = acc - mu
    var = jnp.mean(xc * xc, axis=-1, keepdims=True)
    ln = xc * lax.rsqrt(var + LN_EPS) * lng_ref[...] + lnb_ref[...]
    y_b = _dot((ln * jax.nn.sigmoid(ln)).astype(BF16), wb_ref[...])
    v_buf[0:HALO_CF, :] = v_buf[tm:tm + HALO_CF, :]

    pu = _dot(h, w_in_ref[:, COL_POOL:COL_POOL + W_BRANCH])
    p_buf[HALO_POOL:HALO_POOL + tm, :] = pu
    pos = (i * tm + lax.broadcasted_iota(jnp.int32, (tm, 1), 0)).astype(F32)
    q_groups = []
    for gi, w in enumerate(POOL_WINDOWS):
        lo, hi = gi * POOL_GROUP, (gi + 1) * POOL_GROUP
        tok = pu[:, lo:hi]
        win = tok
        for j in range(1, w):
            win = win + p_buf[HALO_POOL - j:HALO_POOL - j + tm, lo:hi]
        inv_count = 1.0 / jnp.minimum(pos + 1.0, float(w))
        q_groups.append(_dot((win * inv_count - tok).astype(BF16), poolw_ref[gi]))
    q = jnp.concatenate(q_groups, axis=-1) * pscale_ref[...]
    y_c = _dot(q.astype(BF16), wc_ref[...])
    p_buf[0:HALO_POOL, :] = p_buf[tm:tm + HALO_POOL, :]

    gates = jax.nn.sigmoid(_dot(h, w_in_ref[:, COL_GATE:COL_GATE + 3 * D_MODEL]))
    merged = (gates[:, 0:D_MODEL] * y_a + gates[:, D_MODEL:2 * D_MODEL] * y_b
              + gates[:, 2 * D_MODEL:3 * D_MODEL] * y_c)
    out_ref[...] = x + _dot(merged.astype(BF16), wo_ref[...])


def _resident(shape):
    zeros = (0,) * len(shape)
    return pl.BlockSpec(shape, lambda b, i: zeros, pipeline_mode=pl.Buffered(1))


def _mixer(x, g, w_in, scw, cfw, cfb, lng, lnb, poolw, pscale, wa, wb, wc, wo):
    batch, seq, _ = x.shape
    tm = MIX_TM
    row = lambda v: v.reshape(1, -1)
    tile = pl.BlockSpec((None, tm, D_MODEL), lambda b, i: (b, i, 0))
    operands = (row(g), w_in.astype(BF16), scw, cfw, row(cfb), row(lng), row(lnb),
                poolw.astype(BF16), row(pscale), wa.astype(BF16), wb.astype(BF16),
                wc.astype(BF16), wo.astype(BF16))
    return pl.pallas_call(
        _mixer_kernel,
        grid=(batch, seq // tm),
        in_specs=[tile] + [_resident(op.shape) for op in operands],
        out_specs=tile,
        out_shape=jax.ShapeDtypeStruct(x.shape, F32),
        scratch_shapes=[pltpu.VMEM((HALO_SC + tm, W_BRANCH), F32),
                        pltpu.VMEM((HALO_CF + tm, W_BRANCH), F32),
                        pltpu.VMEM((HALO_POOL + tm, W_BRANCH), F32)],
        compiler_params=pltpu.CompilerParams(
            dimension_semantics=("arbitrary", "arbitrary"),
            vmem_limit_bytes=VMEM_LIMIT),
        name="mixer",
    )(x, *operands)


def _swiglu_step(h, w1_ref, w3_ref, w2_ref):
    a = _dot(h, w1_ref[...])
    c = _dot(h, w3_ref[...])
    return _dot((a * jax.nn.sigmoid(a) * c).astype(BF16), w2_ref[...])


def _dense_ffn_kernel(x_ref, g_ref, w1_ref, w3_ref, w2_ref, out_ref, h_buf, acc_buf):
    f = pl.program_id(1)

    @pl.when(f == 0)
    def _():
        x = x_ref[...]
        h_buf[...] = _rms_norm(x, g_ref[...]).astype(BF16)
        acc_buf[...] = x

    acc_buf[...] += _swiglu_step(h_buf[...], w1_ref, w3_ref, w2_ref)

    @pl.when(f == pl.num_programs(1) - 1)
    def _():
        out_ref[...] = acc_buf[...]


def _dense_ffn(x2d, g, w1, w3, w2):
    tokens = x2d.shape[0]
    tm, fc = FFN_TM, FFN_FC
    tile = pl.BlockSpec((tm, D_MODEL), lambda i, f: (i, 0))
    return pl.pallas_call(
        _dense_ffn_kernel,
        grid=(tokens // tm, D_FF // fc),
        in_specs=[tile,
                  pl.BlockSpec((1, D_MODEL), lambda i, f: (0, 0)),
                  pl.BlockSpec((D_MODEL, fc), lambda i, f: (0, f)),
                  pl.BlockSpec((D_MODEL, fc), lambda i, f: (0, f)),
                  pl.BlockSpec((fc, D_MODEL), lambda i, f: (f, 0))],
        out_specs=tile,
        out_shape=jax.ShapeDtypeStruct(x2d.shape, F32),
        scratch_shapes=[pltpu.VMEM((tm, D_MODEL), BF16), pltpu.VMEM((tm, D_MODEL), F32)],
        compiler_params=pltpu.CompilerParams(
            dimension_semantics=("arbitrary", "arbitrary"),
            vmem_limit_bytes=VMEM_LIMIT),
        name="dense_ffn",
    )(x2d, g.reshape(1, -1), w1.astype(BF16), w3.astype(BF16), w2.astype(BF16))


def _moe_kernel(x_ref, g_ref, gf_ref, router_ref, w1_ref, w3_ref, w2_ref, out_ref,
                h_buf, acc_buf, comb_buf, *, final_norm):
    e = pl.program_id(1)
    f = pl.program_id(2)
    first = jnp.logical_and(e == 0, f == 0)
    last = jnp.logical_and(e == pl.num_programs(1) - 1, f == pl.num_programs(2) - 1)

    @pl.when(first)
    def _():
        x = x_ref[...]
        h = _rms_norm(x, g_ref[...])
        h_buf[...] = h.astype(BF16)
        acc_buf[...] = x
        logits = jnp.dot(h, router_ref[...], preferred_element_type=F32,
                         precision=lax.Precision.HIGHEST)
        lane = lax.broadcasted_iota(jnp.int32, logits.shape, 1)
        m1 = jnp.max(logits, axis=-1, keepdims=True)
        i1 = jnp.min(jnp.where(logits == m1, lane, N_EXPERTS), axis=-1, keepdims=True)
        rest = jnp.where(lane == i1, -jnp.inf, logits)
        m2 = jnp.max(rest, axis=-1, keepdims=True)
        i2 = jnp.min(jnp.where(rest == m2, lane, N_EXPERTS), axis=-1, keepdims=True)
        e2 = jnp.exp(m2 - m1)
        denom = 1.0 + e2
        comb_buf[...] = (jnp.where(lane == i1, 1.0 / denom, 0.0)
                         + jnp.where(lane == i2, e2 / denom, 0.0))

    comb = comb_buf[...]
    lane = lax.broadcasted_iota(jnp.int32, comb.shape, 1)
    weight = jnp.sum(jnp.where(lane == e, comb, 0.0), axis=-1, keepdims=True)
    acc_buf[...] += weight * _swiglu_step(h_buf[...], w1_ref, w3_ref, w2_ref)

    @pl.when(last)
    def _():
        y = acc_buf[...]
        out_ref[...] = _rms_norm(y, gf_ref[...]) if final_norm else y


def _moe_ffn(x2d, g, g_final, router, w1, w3, w2, *, final_norm):
    tokens = x2d.shape[0]
    tm, fc = FFN_TM, FFN_FC
    tile = pl.BlockSpec((tm, D_MODEL), lambda i, e, f: (i, 0))
    vec = pl.BlockSpec((1, D_MODEL), lambda i, e, f: (0, 0))
    return pl.pallas_call(
        functools.partial(_moe_kernel, final_norm=final_norm),
        grid=(tokens // tm, N_EXPERTS, D_FF // fc),
        in_specs=[tile, vec, vec,
                  pl.BlockSpec((D_MODEL, N_EXPERTS), lambda i, e, f: (0, 0)),
                  pl.BlockSpec((None, D_MODEL, fc), lambda i, e, f: (e, 0, f)),
                  pl.BlockSpec((None, D_MODEL, fc), lambda i, e, f: (e, 0, f)),
                  pl.BlockSpec((None, fc, D_MODEL), lambda i, e, f: (e, f, 0))],
        out_specs=tile,
        out_shape=jax.ShapeDtypeStruct(x2d.shape, F32),
        scratch_shapes=[pltpu.VMEM((tm, D_MODEL), BF16), pltpu.VMEM((tm, D_MODEL), F32),
                        pltpu.VMEM((tm, N_EXPERTS), F32)],
        compiler_params=pltpu.CompilerParams(
            dimension_semantics=("arbitrary", "arbitrary", "arbitrary"),
            vmem_limit_bytes=VMEM_LIMIT),
        name="moe_ffn",
    )(x2d, g.reshape(1, -1), g_final.reshape(1, -1), router,
      w1.astype(BF16), w3.astype(BF16), w2.astype(BF16))


def kernel(x, norm_mix_g, w_in, sc_conv_w, cf_conv_w, cf_conv_b, cf_ln_g, cf_ln_b,
           pool_w, pool_scale, w_sc_out, w_cf_out, w_pool_out, w_o, norm_ffn_g,
           dense_w1, dense_w3, dense_w2, moe_router, moe_w1, moe_w3, moe_w2,
           norm_final_g):
    batch, seq, d = x.shape
    depth = w_in.shape[0]
    for layer in range(depth):
        x = _mixer(x, norm_mix_g[layer], w_in[layer], sc_conv_w[layer], cf_conv_w[layer],
                   cf_conv_b[layer], cf_ln_g[layer], cf_ln_b[layer], pool_w[layer],
                   pool_scale[layer], w_sc_out[layer], w_cf_out[layer], w_pool_out[layer],
                   w_o[layer])
        x2d = x.reshape(batch * seq, d)
        i = layer // 2
        if layer % 2 == 0:
            x2d = _dense_ffn(x2d, norm_ffn_g[layer], dense_w1[i], dense_w3[i], dense_w2[i])
        else:
            x2d = _moe_ffn(x2d, norm_ffn_g[layer], norm_final_g, moe_router[i],
                           moe_w1[i], moe_w3[i], moe_w2[i],
                           final_norm=(layer == depth - 1))
        x = x2d.reshape(batch, seq, d)
    if depth % 2 == 1:
        x = _final_norm(x, norm_final_g)
    return x


def _final_norm_kernel(x_ref, g_ref, out_ref):
    out_ref[...] = _rms_norm(x_ref[...], g_ref[...])


def _final_norm(x, g):
    batch, seq, d = x.shape
    x2d = x.reshape(batch * seq, d)
    tile = pl.BlockSpec((FFN_TM, d), lambda i: (i, 0))
    out = pl.pallas_call(
        _final_norm_kernel,
        grid=(x2d.shape[0] // FFN_TM,),
        in_specs=[tile, pl.BlockSpec((1, d), lambda i: (0, 0))],
        out_specs=tile,
        out_shape=jax.ShapeDtypeStruct(x2d.shape, F32),
        name="final_norm",
    )(x2d, g.reshape(1, -1))
    return out.reshape(batch, seq, d)
```

```python
import functools

import jax
import jax.numpy as jnp
from jax import lax
from jax.experimental import pallas as pl
from jax.experimental.pallas import tpu as pltpu

D_MODEL = 1024
W_BRANCH = 1024
SC_KERNEL = 3
CF_KERNEL = 31
POOL_WINDOWS = (2, 4, 8, 16)
POOL_GROUP = W_BRANCH // len(POOL_WINDOWS)
D_FF = 3584
N_EXPERTS = 8
TOP_K = 2
RMS_EPS = 1e-6
LN_EPS = 1e-5

COL_SC = 0
COL_CF = 3 * W_BRANCH
COL_POOL = 5 * W_BRANCH
COL_GATE = 6 * W_BRANCH
IN_COLS = 9 * W_BRANCH

HALO_SC = 8
HALO_CF = 32
HALO_POOL = 16

MIX_TM = 256
FFN_TM = 512
FFN_FC = 512
MOE_TM = 512
ROUTE_TM = 512
COMBINE_TM = 512
VMEM_LIMIT = 60 * 1024 * 1024

F32 = jnp.float32
BF16 = jnp.bfloat16


def _rms_norm(x, g):
    ms = jnp.mean(x * x, axis=-1, keepdims=True)
    return x * lax.rsqrt(ms + RMS_EPS) * g


def _dot(a, b):
    return jnp.dot(a, b, preferred_element_type=F32)


def _mixer_kernel(x_ref, g_ref, w_in_ref, scw_ref, cfw_ref, cfb_ref, lng_ref, lnb_ref,
                  poolw_ref, pscale_ref, wa_ref, wb_ref, wc_ref, wo_ref,
                  out_ref, u_buf, v_buf, p_buf):
    i = pl.program_id(1)
    tm = x_ref.shape[0]

    @pl.when(i == 0)
    def _():
        u_buf[0:HALO_SC, :] = jnp.zeros((HALO_SC, W_BRANCH), F32)
        v_buf[0:HALO_CF, :] = jnp.zeros((HALO_CF, W_BRANCH), F32)
        p_buf[0:HALO_POOL, :] = jnp.zeros((HALO_POOL, W_BRANCH), F32)

    x = x_ref[...]
    h = _rms_norm(x, g_ref[...]).astype(BF16)

    pa = _dot(h, w_in_ref[:, COL_SC:COL_SC + 3 * W_BRANCH])
    u = pa[:, W_BRANCH:2 * W_BRANCH] * pa[:, 2 * W_BRANCH:3 * W_BRANCH]
    u_buf[HALO_SC:HALO_SC + tm, :] = u
    conv = scw_ref[SC_KERNEL - 1:SC_KERNEL, :] * u
    for k in range(SC_KERNEL - 1):
        off = HALO_SC - (SC_KERNEL - 1) + k
        conv = conv + scw_ref[k:k + 1, :] * u_buf[off:off + tm, :]
    y_a = _dot((pa[:, 0:W_BRANCH] * conv).astype(BF16), wa_ref[...])
    u_buf[0:HALO_SC, :] = u_buf[tm:tm + HALO_SC, :]

    pb = _dot(h, w_in_ref[:, COL_CF:COL_CF + 2 * W_BRANCH])
    v = pb[:, 0:W_BRANCH] * jax.nn.sigmoid(pb[:, W_BRANCH:2 * W_BRANCH])
    v_buf[HALO_CF:HALO_CF + tm, :] = v
    acc = cfw_ref[CF_KERNEL - 1:CF_KERNEL, :] * v + cfb_ref[...]
    for k in range(CF_KERNEL - 1):
        off = HALO_CF - (CF_KERNEL - 1) + k
        acc = acc + cfw_ref[k:k + 1, :] * v_buf[off:off + tm, :]
    mu = jnp.mean(acc, axis=-1, keepdims=True)
    xc = acc - mu
    var = jnp.mean(xc * xc, axis=-1, keepdims=True)
    ln = xc * lax.rsqrt(var + LN_EPS) * lng_ref[...] + lnb_ref[...]
    y_b = _dot((ln * jax.nn.sigmoid(ln)).astype(BF16), wb_ref[...])
    v_buf[0:HALO_CF, :] = v_buf[tm:tm + HALO_CF, :]

    pu = _dot(h, w_in_ref[:, COL_POOL:COL_POOL + W_BRANCH])
    p_buf[HALO_POOL:HALO_POOL + tm, :] = pu
    pos = (i * tm + lax.broadcasted_iota(jnp.int32, (tm, 1), 0)).astype(F32)
    q_groups = []
    for gi, w in enumerate(POOL_WINDOWS):
        lo, hi = gi * POOL_GROUP, (gi + 1) * POOL_GROUP
        tok = pu[:, lo:hi]
        win = tok
        for j in range(1, w):
            win = win + p_buf[HALO_POOL - j:HALO_POOL - j + tm, lo:hi]
        inv_count = 1.0 / jnp.minimum(pos + 1.0, float(w))
        q_groups.append(_dot((win * inv_count - tok).astype(BF16), poolw_ref[gi]))
    q = jnp.concatenate(q_groups, axis=-1) * pscale_ref[...]
    y_c = _dot(q.astype(BF16), wc_ref[...])
    p_buf[0:HALO_POOL, :] = p_buf[tm:tm + HALO_POOL, :]

    gates = jax.nn.sigmoid(_dot(h, w_in_ref[:, COL_GATE:COL_GATE + 3 * D_MODEL]))
    merged = (gates[:, 0:D_MODEL] * y_a + gates[:, D_MODEL:2 * D_MODEL] * y_b
              + gates[:, 2 * D_MODEL:3 * D_MODEL] * y_c)
    out_ref[...] = x + _dot(merged.astype(BF16), wo_ref[...])


def _resident(shape):
    zeros = (0,) * len(shape)
    return pl.BlockSpec(shape, lambda b, i: zeros, pipeline_mode=pl.Buffered(1))


def _mixer(x, g, w_in, scw, cfw, cfb, lng, lnb, poolw, pscale, wa, wb, wc, wo):
    batch, seq, _ = x.shape
    tm = MIX_TM
    row = lambda v: v.reshape(1, -1)
    tile = pl.BlockSpec((None, tm, D_MODEL), lambda b, i: (b, i, 0))
    operands = (row(g), w_in.astype(BF16), scw, cfw, row(cfb), row(lng), row(lnb),
                poolw.astype(BF16), row(pscale), wa.astype(BF16), wb.astype(BF16),
                wc.astype(BF16), wo.astype(BF16))
    return pl.pallas_call(
        _mixer_kernel,
        grid=(batch, seq // tm),
        in_specs=[tile] + [_resident(op.shape) for op in operands],
        out_specs=tile,
        out_shape=jax.ShapeDtypeStruct(x.shape, F32),
        scratch_shapes=[pltpu.VMEM((HALO_SC + tm, W_BRANCH), F32),
                        pltpu.VMEM((HALO_CF + tm, W_BRANCH), F32),
                        pltpu.VMEM((HALO_POOL + tm, W_BRANCH), F32)],
        compiler_params=pltpu.CompilerParams(
            dimension_semantics=("arbitrary", "arbitrary"),
            vmem_limit_bytes=VMEM_LIMIT),
        name="mixer",
    )(x, *operands)


def _swiglu_step(h, w1_ref, w3_ref, w2_ref):
    a = _dot(h, w1_ref[...])
    c = _dot(h, w3_ref[...])
    return _dot((a * jax.nn.sigmoid(a) * c).astype(BF16), w2_ref[...])


def _dense_ffn_kernel(x_ref, g_ref, w1_ref, w3_ref, w2_ref, out_ref, h_buf, acc_buf):
    f = pl.program_id(1)

    @pl.when(f == 0)
    def _():
        x = x_ref[...]
        h_buf[...] = _rms_norm(x, g_ref[...]).astype(BF16)
        acc_buf[...] = x

    acc_buf[...] += _swiglu_step(h_buf[...], w1_ref, w3_ref, w2_ref)

    @pl.when(f == pl.num_programs(1) - 1)
    def _():
        out_ref[...] = acc_buf[...]


def _dense_ffn(x2d, g, w1, w3, w2):
    tokens = x2d.shape[0]
    tm, fc = FFN_TM, FFN_FC
    tile = pl.BlockSpec((tm, D_MODEL), lambda i, f: (i, 0))
    return pl.pallas_call(
        _dense_ffn_kernel,
        grid=(tokens // tm, D_FF // fc),
        in_specs=[tile,
                  pl.BlockSpec((1, D_MODEL), lambda i, f: (0, 0)),
                  pl.BlockSpec((D_MODEL, fc), lambda i, f: (0, f)),
                  pl.BlockSpec((D_MODEL, fc), lambda i, f: (0, f)),
                  pl.BlockSpec((fc, D_MODEL), lambda i, f: (f, 0))],
        out_specs=tile,
        out_shape=jax.ShapeDtypeStruct(x2d.shape, F32),
        scratch_shapes=[pltpu.VMEM((tm, D_MODEL), BF16), pltpu.VMEM((tm, D_MODEL), F32)],
        compiler_params=pltpu.CompilerParams(
            dimension_semantics=("arbitrary", "arbitrary"),
            vmem_limit_bytes=VMEM_LIMIT),
        name="dense_ffn",
    )(x2d, g.reshape(1, -1), w1.astype(BF16), w3.astype(BF16), w2.astype(BF16))


def _router_kernel(x_ref, g_ref, router_ref, ids_ref, probs_ref):
    h = _rms_norm(x_ref[...], g_ref[...])
    logits = jnp.dot(h, router_ref[...], preferred_element_type=F32,
                     precision=lax.Precision.HIGHEST)
    lane = lax.broadcasted_iota(jnp.int32, logits.shape, 1)
    m1 = jnp.max(logits, axis=-1, keepdims=True)
    i1 = jnp.min(jnp.where(logits == m1, lane, N_EXPERTS), axis=-1, keepdims=True)
    rest = jnp.where(lane == i1, -jnp.inf, logits)
    m2 = jnp.max(rest, axis=-1, keepdims=True)
    i2 = jnp.min(jnp.where(rest == m2, lane, N_EXPERTS), axis=-1, keepdims=True)
    e2 = jnp.exp(m2 - m1)
    denom = 1.0 + e2
    first = lax.broadcasted_iota(jnp.int32, ids_ref.shape, 1) == 0
    ids_ref[...] = jnp.where(first, i1, i2)
    probs_ref[...] = jnp.where(first, 1.0 / denom, e2 / denom)


def _router(x2d, g, router):
    tokens = x2d.shape[0]
    tm = ROUTE_TM
    pair = pl.BlockSpec((tm, TOP_K), lambda i: (i, 0))
    return pl.pallas_call(
        _router_kernel,
        grid=(tokens // tm,),
        in_specs=[pl.BlockSpec((tm, D_MODEL), lambda i: (i, 0)),
                  pl.BlockSpec((1, D_MODEL), lambda i: (0, 0)),
                  pl.BlockSpec((D_MODEL, N_EXPERTS), lambda i: (0, 0))],
        out_specs=[pair, pair],
        out_shape=[jax.ShapeDtypeStruct((tokens, TOP_K), jnp.int32),
                   jax.ShapeDtypeStruct((tokens, TOP_K), F32)],
        compiler_params=pltpu.CompilerParams(dimension_semantics=("arbitrary",)),
        name="router",
    )(x2d, g.reshape(1, -1), router)


def _routing_tables(ids, tm):
    tokens = ids.shape[0]
    n = TOP_K * tokens
    n_tiles = n // tm + N_EXPERTS
    e_flat = ids.T.reshape(n)
    order = jnp.argsort(e_flat, stable=True).astype(jnp.int32)
    experts = jnp.arange(N_EXPERTS, dtype=jnp.int32)
    counts = jnp.sum((e_flat[:, None] == experts[None, :]).astype(jnp.int32), axis=0)
    tiles_per = (counts + tm - 1) // tm
    tile_end = jnp.cumsum(tiles_per)
    n_used = tile_end[-1]
    jt = jnp.arange(n_tiles, dtype=jnp.int32)
    tile_expert = jnp.minimum(
        jnp.sum((jt[:, None] >= tile_end[None, :]).astype(jnp.int32), axis=1), N_EXPERTS - 1)
    group_start = jnp.cumsum(counts) - counts
    first_tile = tile_end - tiles_per
    q0 = group_start[tile_expert] + (jt - first_tile[tile_expert]) * tm
    q_end = (group_start + counts)[tile_expert]
    q = q0[:, None] + jnp.arange(tm, dtype=jnp.int32)[None, :]
    valid = jnp.logical_and(q < q_end[:, None], (jt < n_used)[:, None]).reshape(-1)
    order_padded = jnp.concatenate([order, jnp.zeros((tm,), jnp.int32)])
    row_assign = jax.vmap(lambda s: lax.dynamic_slice(order_padded, (s,), (tm,)))(
        jnp.clip(q0, 0, n)).reshape(-1)
    pad_rank = jnp.cumsum(jnp.logical_not(valid).astype(jnp.int32)) - 1
    row_token = jnp.where(valid, row_assign % tokens, 0).astype(jnp.int32)
    row_dst = jnp.where(valid, row_assign, n + pad_rank).astype(jnp.int32)
    return tile_expert.astype(jnp.int32), n_used.reshape(1).astype(jnp.int32), row_token, row_dst


def _moe_group_kernel(te_ref, nu_ref, tok_ref, dst_ref,
                      x_hbm, g_ref, w1_ref, w3_ref, w2_ref, y_hbm,
                      xbuf, h_buf, acc_buf, obuf, gather_sem, scatter_sem):
    del te_ref
    j = pl.program_id(0)
    f = pl.program_id(1)
    last_f = pl.num_programs(1) - 1
    n_used = nu_ref[0]
    tm = xbuf.shape[0]
    used = j < n_used

    def start_gather(tile):
        base = tile * tm

        def body(r, carry):
            pltpu.make_async_copy(x_hbm.at[pl.ds(tok_ref[base + r], 1), :],
                                  xbuf.at[pl.ds(r, 1), :], gather_sem).start()
            return carry
        lax.fori_loop(0, tm, body, 0)

    def start_scatter(tile):
        base = tile * tm

        def body(r, carry):
            pltpu.make_async_copy(obuf.at[pl.ds(r, 1), :],
                                  y_hbm.at[pl.ds(dst_ref[base + r], 1), :], scatter_sem).start()
            return carry
        lax.fori_loop(0, tm, body, 0)

    def wait_gather():
        pltpu.make_async_copy(x_hbm.at[pl.ds(0, tm), :], xbuf, gather_sem).wait()

    def wait_scatter():
        pltpu.make_async_copy(obuf, y_hbm.at[pl.ds(0, tm), :], scatter_sem).wait()

    @pl.when(jnp.logical_and(f == 0, j == 0))
    def _():
        start_gather(0)

    @pl.when(jnp.logical_and(f == 0, used))
    def _():
        wait_gather()
        h_buf[...] = _rms_norm(xbuf[...], g_ref[...]).astype(BF16)
        acc_buf[...] = jnp.zeros_like(acc_buf)

    @pl.when(jnp.logical_and(f == 0, j + 1 < n_used))
    def _():
        start_gather(j + 1)

    @pl.when(used)
    def _():
        acc_buf[...] += _swiglu_step(h_buf[...], w1_ref, w3_ref, w2_ref)

    @pl.when(jnp.logical_and(f == last_f, used))
    def _():
        @pl.when(j > 0)
        def _():
            wait_scatter()
        obuf[...] = acc_buf[...]
        start_scatter(j)

        @pl.when(j == n_used - 1)
        def _():
            wait_scatter()

    @pl.when(jnp.logical_and(f == 0, jnp.logical_not(used)))
    def _():
        obuf[...] = jnp.zeros_like(obuf)
        fill = pltpu.make_async_copy(obuf, y_hbm.at[pl.ds(pl.multiple_of(j * tm, tm), tm), :],
                                     scatter_sem)
        fill.start()
        fill.wait()


def _moe_grouped(x2d, g, w1, w3, w2, tables):
    tile_expert, n_used, row_token, row_dst = tables
    tm, fc = MOE_TM, FFN_FC
    n_rows = row_token.shape[0]
    grid_spec = pltpu.PrefetchScalarGridSpec(
        num_scalar_prefetch=4,
        grid=(n_rows // tm, D_FF // fc),
        in_specs=[pl.BlockSpec(memory_space=pl.ANY),
                  pl.BlockSpec((1, D_MODEL), lambda j, f, te, nu, tok, dst: (0, 0)),
                  pl.BlockSpec((None, D_MODEL, fc), lambda j, f, te, nu, tok, dst: (te[j], 0, f)),
                  pl.BlockSpec((None, D_MODEL, fc), lambda j, f, te, nu, tok, dst: (te[j], 0, f)),
                  pl.BlockSpec((None, fc, D_MODEL), lambda j, f, te, nu, tok, dst: (te[j], f, 0))],
        out_specs=pl.BlockSpec(memory_space=pl.ANY),
        scratch_shapes=[pltpu.VMEM((tm, D_MODEL), F32), pltpu.VMEM((tm, D_MODEL), BF16),
                        pltpu.VMEM((tm, D_MODEL), F32), pltpu.VMEM((tm, D_MODEL), F32),
                        pltpu.SemaphoreType.DMA(()), pltpu.SemaphoreType.DMA(())],
    )
    return pl.pallas_call(
        _moe_group_kernel,
        grid_spec=grid_spec,
        out_shape=jax.ShapeDtypeStruct((n_rows, D_MODEL), F32),
        compiler_params=pltpu.CompilerParams(
            dimension_semantics=("arbitrary", "arbitrary"),
            vmem_limit_bytes=VMEM_LIMIT),
        name="moe_grouped",
    )(tile_expert, n_used, row_token, row_dst, x2d, g.reshape(1, -1),
      w1.astype(BF16), w3.astype(BF16), w2.astype(BF16))


def _combine_kernel(x_ref, y0_ref, y1_ref, p_ref, gf_ref, out_ref, *, final_norm):
    p = p_ref[...]
    y = x_ref[...] + p[:, 0:1] * y0_ref[...] + p[:, 1:2] * y1_ref[...]
    out_ref[...] = _rms_norm(y, gf_ref[...]) if final_norm else y


def _combine(x2d, y_rows, probs, g_final, *, final_norm):
    tokens = x2d.shape[0]
    tm = COMBINE_TM
    tile = pl.BlockSpec((tm, D_MODEL), lambda i: (i, 0))
    second = pl.BlockSpec((tm, D_MODEL), lambda i: (i + tokens // tm, 0))
    return pl.pallas_call(
        functools.partial(_combine_kernel, final_norm=final_norm),
        grid=(tokens // tm,),
        in_specs=[tile, tile, second,
                  pl.BlockSpec((tm, TOP_K), lambda i: (i, 0)),
                  pl.BlockSpec((1, D_MODEL), lambda i: (0, 0))],
        out_specs=tile,
        out_shape=jax.ShapeDtypeStruct(x2d.shape, F32),
        compiler_params=pltpu.CompilerParams(dimension_semantics=("arbitrary",)),
        name="moe_combine",
    )(x2d, y_rows, y_rows, probs, g_final.reshape(1, -1))


def _moe_ffn(x2d, g, g_final, router, w1, w3, w2, *, final_norm):
    ids, probs = _router(x2d, g, router)
    tables = _routing_tables(ids, MOE_TM)
    y_rows = _moe_grouped(x2d, g, w1, w3, w2, tables)
    return _combine(x2d, y_rows, probs, g_final, final_norm=final_norm)


def kernel(x, norm_mix_g, w_in, sc_conv_w, cf_conv_w, cf_conv_b, cf_ln_g, cf_ln_b,
           pool_w, pool_scale, w_sc_out, w_cf_out, w_pool_out, w_o, norm_ffn_g,
           dense_w1, dense_w3, dense_w2, moe_router, moe_w1, moe_w3, moe_w2,
           norm_final_g):
    batch, seq, d = x.shape
    depth = w_in.shape[0]
    for layer in range(depth):
        x = _mixer(x, norm_mix_g[layer], w_in[layer], sc_conv_w[layer], cf_conv_w[layer],
                   cf_conv_b[layer], cf_ln_g[layer], cf_ln_b[layer], pool_w[layer],
                   pool_scale[layer], w_sc_out[layer], w_cf_out[layer], w_pool_out[layer],
                   w_o[layer])
        x2d = x.reshape(batch * seq, d)
        i = layer // 2
        if layer % 2 == 0:
            x2d = _dense_ffn(x2d, norm_ffn_g[layer], dense_w1[i], dense_w3[i], dense_w2[i])
        else:
            x2d = _moe_ffn(x2d, norm_ffn_g[layer], norm_final_g, moe_router[i],
                           moe_w1[i], moe_w3[i], moe_w2[i],
                           final_norm=(layer == depth - 1))
        x = x2d.reshape(batch, seq, d)
    if depth % 2 == 1:
        x = _final_norm(x, norm_final_g)
    return x


def _final_norm_kernel(x_ref, g_ref, out_ref):
    out_ref[...] = _rms_norm(x_ref[...], g_ref[...])


def _final_norm(x, g):
    batch, seq, d = x.shape
    x2d = x.reshape(batch * seq, d)
    tile = pl.BlockSpec((FFN_TM, d), lambda i: (i, 0))
    out = pl.pallas_call(
        _final_norm_kernel,
        grid=(x2d.shape[0] // FFN_TM,),
        in_specs=[tile, pl.BlockSpec((1, d), lambda i: (0, 0))],
        out_specs=tile,
        out_shape=jax.ShapeDtypeStruct(x2d.shape, F32),
        name="final_norm",
    )(x2d, g.reshape(1, -1))
    return out.reshape(batch, seq, d)
```

```python
import functools

import jax
import jax.numpy as jnp
from jax import lax
from jax.experimental import pallas as pl
from jax.experimental.pallas import tpu as pltpu

D_MODEL = 1024
W_BRANCH = 1024
SC_KERNEL = 3
CF_KERNEL = 31
POOL_WINDOWS = (2, 4, 8, 16)
POOL_GROUP = W_BRANCH // len(POOL_WINDOWS)
D_FF = 3584
N_EXPERTS = 8
TOP_K = 2
RMS_EPS = 1e-6
LN_EPS = 1e-5

COL_SC = 0
COL_CF = 3 * W_BRANCH
COL_POOL = 5 * W_BRANCH
COL_GATE = 6 * W_BRANCH
IN_COLS = 9 * W_BRANCH

SUBLANES = 8
LANES = 128
ROW_TILES = D_MODEL // LANES
assert ROW_TILES == SUBLANES
assert POOL_WINDOWS == tuple(2 ** (g + 1) for g in range(len(POOL_WINDOWS)))

HALO_SC = SUBLANES
HALO_CF = SUBLANES * (-(-CF_KERNEL // SUBLANES))
HALO_POOL = SUBLANES * len(POOL_WINDOWS)

MIX_TM = 256
FFN_TM = 512
FFN_FC = 512
MOE_TM = 512
ROUTE_TM = 512
COMBINE_TM = 512
VMEM_LIMIT = 60 * 1024 * 1024

F32 = jnp.float32
BF16 = jnp.bfloat16


def _rms_norm(x, g):
    ms = jnp.mean(x * x, axis=-1, keepdims=True)
    return x * lax.rsqrt(ms + RMS_EPS) * g


def _dot(a, b):
    return jnp.dot(a, b, preferred_element_type=F32)


def _mixer_kernel(x_ref, g_ref, w_in_ref, scw_ref, cfw_ref, cfb_ref, lng_ref, lnb_ref,
                  poolw_ref, pscale_ref, wa_ref, wb_ref, wc_ref, wo_ref,
                  out_ref, u_buf, v_buf, p_buf):
    i = pl.program_id(1)
    tm = x_ref.shape[0]

    @pl.when(i == 0)
    def _():
        u_buf[0:HALO_SC, :] = jnp.zeros((HALO_SC, W_BRANCH), F32)
        v_buf[0:HALO_CF, :] = jnp.zeros((HALO_CF, W_BRANCH), F32)
        p_buf[0:HALO_POOL, :] = jnp.zeros((HALO_POOL, W_BRANCH), F32)

    x = x_ref[...]
    h = _rms_norm(x, g_ref[...]).astype(BF16)

    pa = _dot(h, w_in_ref[:, COL_SC:COL_SC + 3 * W_BRANCH])
    u = pa[:, W_BRANCH:2 * W_BRANCH] * pa[:, 2 * W_BRANCH:3 * W_BRANCH]
    u_buf[HALO_SC:HALO_SC + tm, :] = u
    conv = scw_ref[SC_KERNEL - 1:SC_KERNEL, :] * u
    for k in range(SC_KERNEL - 1):
        off = HALO_SC - (SC_KERNEL - 1) + k
        conv = conv + scw_ref[k:k + 1, :] * u_buf[off:off + tm, :]
    y_a = _dot((pa[:, 0:W_BRANCH] * conv).astype(BF16), wa_ref[...])
    u_buf[0:HALO_SC, :] = u_buf[tm:tm + HALO_SC, :]

    pb = _dot(h, w_in_ref[:, COL_CF:COL_CF + 2 * W_BRANCH])
    v = pb[:, 0:W_BRANCH] * jax.nn.sigmoid(pb[:, W_BRANCH:2 * W_BRANCH])
    v_buf[HALO_CF:HALO_CF + tm, :] = v
    ext = tm + SUBLANES
    acc = cfb_ref[...]
    for r in range(SUBLANES):
        part = None
        for q in range(-(-CF_KERNEL // SUBLANES)):
            d = SUBLANES * q + r
            if d < CF_KERNEL:
                base = HALO_CF - SUBLANES * (q + 1)
                term = cfw_ref[CF_KERNEL - 1 - d:CF_KERNEL - d, :] * v_buf[base:base + ext, :]
                part = term if part is None else part + term
        acc = acc + part[SUBLANES - r:SUBLANES - r + tm, :]
    mu = jnp.mean(acc, axis=-1, keepdims=True)
    xc = acc - mu
    var = jnp.mean(xc * xc, axis=-1, keepdims=True)
    ln = xc * lax.rsqrt(var + LN_EPS) * lng_ref[...] + lnb_ref[...]
    y_b = _dot((ln * jax.nn.sigmoid(ln)).astype(BF16), wb_ref[...])
    v_buf[0:HALO_CF, :] = v_buf[tm:tm + HALO_CF, :]

    pu = _dot(h, w_in_ref[:, COL_POOL:COL_POOL + W_BRANCH])
    p_buf[HALO_POOL:HALO_POOL + tm, :] = pu
    pos = (i * tm + lax.broadcasted_iota(jnp.int32, (tm, 1), 0)).astype(F32)
    win_sums = {}
    level, w = p_buf[...], 1
    for gi in range(len(POOL_WINDOWS)):
        level = level[SUBLANES:, :] + level[SUBLANES - w:level.shape[0] - w, :]
        w *= 2
        first_row = HALO_POOL - SUBLANES * (gi + 1)
        win_sums[w] = level[first_row:first_row + tm, 0:POOL_GROUP]
        if gi + 1 < len(POOL_WINDOWS):
            level = level[:, POOL_GROUP:]
    q_groups = []
    for gi, w in enumerate(POOL_WINDOWS):
        tok = pu[:, gi * POOL_GROUP:(gi + 1) * POOL_GROUP]
        inv_count = 1.0 / jnp.minimum(pos + 1.0, float(w))
        q_groups.append(_dot((win_sums[w] * inv_count - tok).astype(BF16), poolw_ref[gi]))
    q = jnp.concatenate(q_groups, axis=-1) * pscale_ref[...]
    y_c = _dot(q.astype(BF16), wc_ref[...])
    p_buf[0:HALO_POOL, :] = p_buf[tm:tm + HALO_POOL, :]

    gates = jax.nn.sigmoid(_dot(h, w_in_ref[:, COL_GATE:COL_GATE + 3 * D_MODEL]))
    merged = (gates[:, 0:D_MODEL] * y_a + gates[:, D_MODEL:2 * D_MODEL] * y_b
              + gates[:, 2 * D_MODEL:3 * D_MODEL] * y_c)
    out_ref[...] = x + _dot(merged.astype(BF16), wo_ref[...])


def _resident(shape):
    zeros = (0,) * len(shape)
    return pl.BlockSpec(shape, lambda b, i: zeros, pipeline_mode=pl.Buffered(1))


def _mixer(x, g, w_in, scw, cfw, cfb, lng, lnb, poolw, pscale, wa, wb, wc, wo):
    batch, seq, _ = x.shape
    tm = MIX_TM
    row = lambda v: v.reshape(1, -1)
    tile = pl.BlockSpec((None, tm, D_MODEL), lambda b, i: (b, i, 0))
    operands = (row(g), w_in.astype(BF16), scw, cfw, row(cfb), row(lng), row(lnb),
                poolw.astype(BF16), row(pscale), wa.astype(BF16), wb.astype(BF16),
                wc.astype(BF16), wo.astype(BF16))
    return pl.pallas_call(
        _mixer_kernel,
        grid=(batch, seq // tm),
        in_specs=[tile] + [_resident(op.shape) for op in operands],
        out_specs=tile,
        out_shape=jax.ShapeDtypeStruct(x.shape, F32),
        scratch_shapes=[pltpu.VMEM((HALO_SC + tm, W_BRANCH), F32),
                        pltpu.VMEM((HALO_CF + tm, W_BRANCH), F32),
                        pltpu.VMEM((HALO_POOL + tm, W_BRANCH), F32)],
        compiler_params=pltpu.CompilerParams(
            dimension_semantics=("arbitrary", "arbitrary"),
            vmem_limit_bytes=VMEM_LIMIT),
        name="mixer",
    )(x, *operands)


def _swiglu_step(h, w1_ref, w3_ref, w2_ref):
    a = _dot(h, w1_ref[...])
    c = _dot(h, w3_ref[...])
    return _dot((a * jax.nn.sigmoid(a) * c).astype(BF16), w2_ref[...])


def _dense_ffn_kernel(x_ref, g_ref, w1_ref, w3_ref, w2_ref, out_ref, h_buf, acc_buf):
    f = pl.program_id(1)

    @pl.when(f == 0)
    def _():
        x = x_ref[...]
        h_buf[...] = _rms_norm(x, g_ref[...]).astype(BF16)
        acc_buf[...] = x

    acc_buf[...] += _swiglu_step(h_buf[...], w1_ref, w3_ref, w2_ref)

    @pl.when(f == pl.num_programs(1) - 1)
    def _():
        out_ref[...] = acc_buf[...]


def _dense_ffn(x2d, g, w1, w3, w2):
    tokens = x2d.shape[0]
    tm, fc = FFN_TM, FFN_FC
    tile = pl.BlockSpec((tm, D_MODEL), lambda i, f: (i, 0))
    return pl.pallas_call(
        _dense_ffn_kernel,
        grid=(tokens // tm, D_FF // fc),
        in_specs=[tile,
                  pl.BlockSpec((1, D_MODEL), lambda i, f: (0, 0)),
                  pl.BlockSpec((D_MODEL, fc), lambda i, f: (0, f)),
                  pl.BlockSpec((D_MODEL, fc), lambda i, f: (0, f)),
                  pl.BlockSpec((fc, D_MODEL), lambda i, f: (f, 0))],
        out_specs=tile,
        out_shape=jax.ShapeDtypeStruct(x2d.shape, F32),
        scratch_shapes=[pltpu.VMEM((tm, D_MODEL), BF16), pltpu.VMEM((tm, D_MODEL), F32)],
        compiler_params=pltpu.CompilerParams(
            dimension_semantics=("arbitrary", "arbitrary"),
            vmem_limit_bytes=VMEM_LIMIT),
        name="dense_ffn",
    )(x2d, g.reshape(1, -1), w1.astype(BF16), w3.astype(BF16), w2.astype(BF16))


def _store_token_tiles(ref, value):
    rows = value.shape[0]
    for c in range(ROW_TILES):
        ref[pl.ds(c, rows, stride=ROW_TILES), :] = value[:, c * LANES:(c + 1) * LANES]


def _load_token_tiles(ref, rows):
    return [ref[pl.ds(c, rows, stride=ROW_TILES), :] for c in range(ROW_TILES)]


def _router_kernel(x_ref, g_ref, router_ref, ids_ref, probs_ref, ht_ref):
    h = _rms_norm(x_ref[...], g_ref[...])
    _store_token_tiles(ht_ref, h)
    logits = jnp.dot(h, router_ref[...], preferred_element_type=F32,
                     precision=lax.Precision.HIGHEST)
    lane = lax.broadcasted_iota(jnp.int32, logits.shape, 1)
    m1 = jnp.max(logits, axis=-1, keepdims=True)
    i1 = jnp.min(jnp.where(logits == m1, lane, N_EXPERTS), axis=-1, keepdims=True)
    rest = jnp.where(lane == i1, -jnp.inf, logits)
    m2 = jnp.max(rest, axis=-1, keepdims=True)
    i2 = jnp.min(jnp.where(rest == m2, lane, N_EXPERTS), axis=-1, keepdims=True)
    e2 = jnp.exp(m2 - m1)
    denom = 1.0 + e2
    first = lax.broadcasted_iota(jnp.int32, ids_ref.shape, 1) == 0
    ids_ref[...] = jnp.where(first, i1, i2)
    probs_ref[...] = jnp.where(first, 1.0 / denom, e2 / denom)


def _router(x2d, g, router):
    tokens = x2d.shape[0]
    tm = ROUTE_TM
    pair = pl.BlockSpec((tm, TOP_K), lambda i: (i, 0))
    return pl.pallas_call(
        _router_kernel,
        grid=(tokens // tm,),
        in_specs=[pl.BlockSpec((tm, D_MODEL), lambda i: (i, 0)),
                  pl.BlockSpec((1, D_MODEL), lambda i: (0, 0)),
                  pl.BlockSpec((D_MODEL, N_EXPERTS), lambda i: (0, 0))],
        out_specs=[pair, pair, pl.BlockSpec((tm * ROW_TILES, LANES), lambda i: (i, 0))],
        out_shape=[jax.ShapeDtypeStruct((tokens, TOP_K), jnp.int32),
                   jax.ShapeDtypeStruct((tokens, TOP_K), F32),
                   jax.ShapeDtypeStruct((tokens * ROW_TILES, LANES), F32)],
        compiler_params=pltpu.CompilerParams(dimension_semantics=("arbitrary",)),
        name="router",
    )(x2d, g.reshape(1, -1), router)


def _routing_tables(ids, tm):
    tokens = ids.shape[0]
    n = TOP_K * tokens
    n_pad = N_EXPERTS * tm
    n_tiles = (n + n_pad) // tm
    e_flat = ids.T.reshape(n)
    experts = jnp.arange(N_EXPERTS, dtype=jnp.int32)
    counts = jnp.sum((e_flat[:, None] == experts[None, :]).astype(jnp.int32), axis=0)
    tiles_per = (counts + tm - 1) // tm
    tile_end = jnp.cumsum(tiles_per)
    n_used = tile_end[-1]
    jt = jnp.arange(n_tiles, dtype=jnp.int32)
    tile_expert = jnp.minimum(
        jnp.sum((jt[:, None] >= tile_end[None, :]).astype(jnp.int32), axis=1), N_EXPERTS - 1)
    pad_end = jnp.cumsum(tiles_per * tm - counts)
    pad_key = jnp.sum((jnp.arange(n_pad, dtype=jnp.int32)[:, None] >= pad_end[None, :])
                      .astype(jnp.int32), axis=1)
    row_dst = jnp.argsort(jnp.concatenate([e_flat, pad_key]), stable=True).astype(jnp.int32)
    row_token = jnp.where(row_dst < n, row_dst % tokens, 0).astype(jnp.int32)
    return tile_expert.astype(jnp.int32), n_used.reshape(1).astype(jnp.int32), row_token, row_dst


def _moe_group_kernel(te_ref, nu_ref, tok_ref, dst_ref,
                      ht_hbm, w1_ref, w3_ref, w2_ref, y_hbm,
                      xbuf, h_buf, acc_buf, obuf, gather_sem, scatter_sem, *, n_steps):
    del te_ref
    j = pl.program_id(0)
    f = pl.program_id(1)
    n_used = nu_ref[0]
    tm = h_buf.shape[0]
    chunk = tm // n_steps
    tail_rows = range(n_steps * chunk, tm)
    used = j < n_used
    has_prev = j > 0
    has_next = j + 1 < n_used
    slot = lax.rem(j, 2)

    def row_tile(r):
        return pl.ds(pl.multiple_of(r * ROW_TILES, ROW_TILES), ROW_TILES)

    def gather_row(tile, r):
        pltpu.make_async_copy(ht_hbm.at[row_tile(tok_ref[tile * tm + r]), :],
                              xbuf.at[row_tile(r), :], gather_sem).start()

    def scatter_row(tile, r):
        s = lax.rem(tile, 2)
        pltpu.make_async_copy(obuf.at[s].at[row_tile(r), :],
                              y_hbm.at[row_tile(dst_ref[tile * tm + r]), :],
                              scatter_sem.at[s]).start()

    def wait_gather():
        pltpu.make_async_copy(ht_hbm.at[pl.ds(0, tm * ROW_TILES), :], xbuf, gather_sem).wait()

    def wait_scatter(s):
        pltpu.make_async_copy(obuf.at[s], y_hbm.at[pl.ds(0, tm * ROW_TILES), :],
                              scatter_sem.at[s]).wait()

    def for_all_rows(row_fn, tile):
        def body(r, carry):
            row_fn(tile, r)
            return carry
        lax.fori_loop(0, tm, body, 0)

    @pl.when(jnp.logical_and(f == 0, j == 0))
    def _():
        for_all_rows(gather_row, 0)

    @pl.when(jnp.logical_and(f == 0, used))
    def _():
        wait_gather()
        for c, part in enumerate(_load_token_tiles(xbuf, tm)):
            h_buf[:, c * LANES:(c + 1) * LANES] = part.astype(BF16)
        acc_buf[...] = jnp.zeros_like(acc_buf)

    @pl.when(jnp.logical_and(f == 0, has_next))
    def _():
        for r in tail_rows:
            gather_row(j + 1, r)

    @pl.when(jnp.logical_and(f == 0, jnp.logical_and(used, has_prev)))
    def _():
        for r in tail_rows:
            scatter_row(j - 1, r)

    def step(do_gather, do_scatter):
        base = f * chunk
        for k in range(chunk):
            if do_gather:
                gather_row(j + 1, base + k)
            if do_scatter:
                scatter_row(j - 1, base + k)
        acc_buf[...] += _swiglu_step(h_buf[...], w1_ref, w3_ref, w2_ref)

    @pl.when(jnp.logical_and(has_prev, has_next))
    def _():
        step(True, True)

    @pl.when(jnp.logical_and(jnp.logical_not(has_prev), has_next))
    def _():
        step(True, False)

    @pl.when(jnp.logical_and(used, jnp.logical_and(has_prev, jnp.logical_not(has_next))))
    def _():
        step(False, True)

    @pl.when(jnp.logical_and(used, jnp.logical_not(jnp.logical_or(has_prev, has_next))))
    def _():
        step(False, False)

    @pl.when(jnp.logical_and(f == n_steps - 1, used))
    def _():
        @pl.when(j >= 2)
        def _():
            wait_scatter(slot)
        _store_token_tiles(obuf.at[slot], acc_buf[...])

        @pl.when(jnp.logical_not(has_next))
        def _():
            for_all_rows(scatter_row, j)
            wait_scatter(slot)

            @pl.when(has_prev)
            def _():
                wait_scatter(1 - slot)

    @pl.when(jnp.logical_and(f == 0, jnp.logical_not(used)))
    def _():
        obuf[0] = jnp.zeros(obuf.shape[1:], F32)
        rows = tm * ROW_TILES
        fill = pltpu.make_async_copy(obuf.at[0],
                                     y_hbm.at[pl.ds(pl.multiple_of(j * rows, rows), rows), :],
                                     scatter_sem.at[0])
        fill.start()
        fill.wait()


def _moe_grouped(ht, w1, w3, w2, tables):
    tile_expert, n_used, row_token, row_dst = tables
    tm, fc = MOE_TM, FFN_FC
    n_rows = row_token.shape[0]
    tile_rows = tm * ROW_TILES
    grid_spec = pltpu.PrefetchScalarGridSpec(
        num_scalar_prefetch=4,
        grid=(n_rows // tm, D_FF // fc),
        in_specs=[pl.BlockSpec(memory_space=pl.ANY),
                  pl.BlockSpec((None, D_MODEL, fc), lambda j, f, te, nu, tok, dst: (te[j], 0, f)),
                  pl.BlockSpec((None, D_MODEL, fc), lambda j, f, te, nu, tok, dst: (te[j], 0, f)),
                  pl.BlockSpec((None, fc, D_MODEL), lambda j, f, te, nu, tok, dst: (te[j], f, 0))],
        out_specs=pl.BlockSpec(memory_space=pl.ANY),
        scratch_shapes=[pltpu.VMEM((tile_rows, LANES), F32), pltpu.VMEM((tm, D_MODEL), BF16),
                        pltpu.VMEM((tm, D_MODEL), F32), pltpu.VMEM((2, tile_rows, LANES), F32),
                        pltpu.SemaphoreType.DMA(()), pltpu.SemaphoreType.DMA((2,))],
    )
    return pl.pallas_call(
        functools.partial(_moe_group_kernel, n_steps=D_FF // fc),
        grid_spec=grid_spec,
        out_shape=jax.ShapeDtypeStruct((n_rows * ROW_TILES, LANES), F32),
        compiler_params=pltpu.CompilerParams(
            dimension_semantics=("arbitrary", "arbitrary"),
            vmem_limit_bytes=VMEM_LIMIT),
        name="moe_grouped",
    )(tile_expert, n_used, row_token, row_dst, ht,
      w1.astype(BF16), w3.astype(BF16), w2.astype(BF16))


def _combine_kernel(x_ref, y0_ref, y1_ref, p_ref, gf_ref, out_ref, *, final_norm):
    tm = x_ref.shape[0]
    p = p_ref[...]
    y0 = _load_token_tiles(y0_ref, tm)
    y1 = _load_token_tiles(y1_ref, tm)
    y = jnp.concatenate(
        [x_ref[:, c * LANES:(c + 1) * LANES] + p[:, 0:1] * y0[c] + p[:, 1:2] * y1[c]
         for c in range(ROW_TILES)], axis=-1)
    out_ref[...] = _rms_norm(y, gf_ref[...]) if final_norm else y


def _combine(x2d, y_tiles, probs, g_final, *, final_norm):
    tokens = x2d.shape[0]
    tm = COMBINE_TM
    tile = pl.BlockSpec((tm, D_MODEL), lambda i: (i, 0))
    first = pl.BlockSpec((tm * ROW_TILES, LANES), lambda i: (i, 0))
    second = pl.BlockSpec((tm * ROW_TILES, LANES), lambda i: (i + tokens // tm, 0))
    return pl.pallas_call(
        functools.partial(_combine_kernel, final_norm=final_norm),
        grid=(tokens // tm,),
        in_specs=[tile, first, second,
                  pl.BlockSpec((tm, TOP_K), lambda i: (i, 0)),
                  pl.BlockSpec((1, D_MODEL), lambda i: (0, 0))],
        out_specs=tile,
        out_shape=jax.ShapeDtypeStruct(x2d.shape, F32),
        compiler_params=pltpu.CompilerParams(dimension_semantics=("arbitrary",)),
        name="moe_combine",
    )(x2d, y_tiles, y_tiles, probs, g_final.reshape(1, -1))


def _moe_ffn(x2d, g, g_final, router, w1, w3, w2, *, final_norm):
    ids, probs, ht = _router(x2d, g, router)
    tables = _routing_tables(ids, MOE_TM)
    y_tiles = _moe_grouped(ht, w1, w3, w2, tables)
    return _combine(x2d, y_tiles, probs, g_final, final_norm=final_norm)


def kernel(x, norm_mix_g, w_in, sc_conv_w, cf_conv_w, cf_conv_b, cf_ln_g, cf_ln_b,
           pool_w, pool_scale, w_sc_out, w_cf_out, w_pool_out, w_o, norm_ffn_g,
           dense_w1, dense_w3, dense_w2, moe_router, moe_w1, moe_w3, moe_w2,
           norm_final_g):
    batch, seq, d = x.shape
    depth = w_in.shape[0]
    for layer in range(depth):
        x = _mixer(x, norm_mix_g[layer], w_in[layer], sc_conv_w[layer], cf_conv_w[layer],
                   cf_conv_b[layer], cf_ln_g[layer], cf_ln_b[layer], pool_w[layer],
                   pool_scale[layer], w_sc_out[layer], w_cf_out[layer], w_pool_out[layer],
                   w_o[layer])
        x2d = x.reshape(batch * seq, d)
        i = layer // 2
        if layer % 2 == 0:
            x2d = _dense_ffn(x2d, norm_ffn_g[layer], dense_w1[i], dense_w3[i], dense_w2[i])
        else:
            x2d = _moe_ffn(x2d, norm_ffn_g[layer], norm_final_g, moe_router[i],
                           moe_w1[i], moe_w3[i], moe_w2[i],
                           final_norm=(layer == depth - 1))
        x = x2d.reshape(batch, seq, d)
    if depth % 2 == 1:
        x = _final_norm(x, norm_final_g)
    return x


def _final_norm_kernel(x_ref, g_ref, out_ref):
    out_ref[...] = _rms_norm(x_ref[...], g_ref[...])


def _final_norm(x, g):
    batch, seq, d = x.shape
    x2d = x.reshape(batch * seq, d)
    tile = pl.BlockSpec((FFN_TM, d), lambda i: (i, 0))
    out = pl.pallas_call(
        _final_norm_kernel,
        grid=(x2d.shape[0] // FFN_TM,),
        in_specs=[tile, pl.BlockSpec((1, d), lambda i: (0, 0))],
        out_specs=tile,
        out_shape=jax.ShapeDtypeStruct(x2d.shape, F32),
        name="final_norm",
    )(x2d, g.reshape(1, -1))
    return out.reshape(batch, seq, d)
```

```python
import functools

import jax
import jax.numpy as jnp
from jax import lax
from jax.experimental import pallas as pl
from jax.experimental.pallas import tpu as pltpu

D_MODEL = 1024
W_BRANCH = 1024
SC_KERNEL = 3
CF_KERNEL = 31
POOL_WINDOWS = (2, 4, 8, 16)
POOL_GROUP = W_BRANCH // len(POOL_WINDOWS)
D_FF = 3584
N_EXPERTS = 8
TOP_K = 2
RMS_EPS = 1e-6
LN_EPS = 1e-5

COL_SC = 0
COL_CF = 3 * W_BRANCH
COL_POOL = 5 * W_BRANCH
COL_GATE = 6 * W_BRANCH
IN_COLS = 9 * W_BRANCH

SUBLANES = 8
LANES = 128
ROW_TILES = D_MODEL // LANES
assert ROW_TILES == SUBLANES
assert POOL_WINDOWS == tuple(2 ** (g + 1) for g in range(len(POOL_WINDOWS)))

HALO_SC = SUBLANES
HALO_CF = SUBLANES * (-(-CF_KERNEL // SUBLANES))
HALO_POOL = SUBLANES * len(POOL_WINDOWS)

MIX_TM = 256
FFN_TM = 512
FFN_FC = 512
MOE_TM = 512
ROUTE_TM = 512
COMBINE_TM = 512
VMEM_LIMIT = 60 * 1024 * 1024

F32 = jnp.float32
BF16 = jnp.bfloat16


def _rms_norm(x, g):
    ms = jnp.mean(x * x, axis=-1, keepdims=True)
    return x * lax.rsqrt(ms + RMS_EPS) * g


def _dot(a, b):
    return jnp.dot(a, b, preferred_element_type=F32)


def _mixer_kernel(x_ref, g_ref, w_in_ref, scw_ref, cfw_ref, cfb_ref, lng_ref, lnb_ref,
                  poolw_ref, pscale_ref, wa_ref, wb_ref, wc_ref, wo_ref,
                  out_ref, u_buf, v_buf, p_buf):
    i = pl.program_id(1)
    tm = x_ref.shape[0]

    @pl.when(i == 0)
    def _():
        u_buf[0:HALO_SC, :] = jnp.zeros((HALO_SC, W_BRANCH), F32)
        v_buf[0:HALO_CF, :] = jnp.zeros((HALO_CF, W_BRANCH), F32)
        p_buf[0:HALO_POOL, :] = jnp.zeros((HALO_POOL, W_BRANCH), F32)

    x = x_ref[...]
    h = _rms_norm(x, g_ref[...]).astype(BF16)

    pa = _dot(h, w_in_ref[:, COL_SC:COL_SC + 3 * W_BRANCH])
    u = pa[:, W_BRANCH:2 * W_BRANCH] * pa[:, 2 * W_BRANCH:3 * W_BRANCH]
    u_buf[HALO_SC:HALO_SC + tm, :] = u
    conv = scw_ref[SC_KERNEL - 1:SC_KERNEL, :] * u
    for k in range(SC_KERNEL - 1):
        off = HALO_SC - (SC_KERNEL - 1) + k
        conv = conv + scw_ref[k:k + 1, :] * u_buf[off:off + tm, :]
    y_a = _dot((pa[:, 0:W_BRANCH] * conv).astype(BF16), wa_ref[...])
    u_buf[0:HALO_SC, :] = u_buf[tm:tm + HALO_SC, :]

    pb = _dot(h, w_in_ref[:, COL_CF:COL_CF + 2 * W_BRANCH])
    v = pb[:, 0:W_BRANCH] * jax.nn.sigmoid(pb[:, W_BRANCH:2 * W_BRANCH])
    v_buf[HALO_CF:HALO_CF + tm, :] = v
    ext = tm + SUBLANES
    acc = cfb_ref[...]
    for r in range(SUBLANES):
        part = None
        for q in range(-(-CF_KERNEL // SUBLANES)):
            d = SUBLANES * q + r
            if d < CF_KERNEL:
                base = HALO_CF - SUBLANES * (q + 1)
                term = cfw_ref[CF_KERNEL - 1 - d:CF_KERNEL - d, :] * v_buf[base:base + ext, :]
                part = term if part is None else part + term
        acc = acc + part[SUBLANES - r:SUBLANES - r + tm, :]
    mu = jnp.mean(acc, axis=-1, keepdims=True)
    xc = acc - mu
    var = jnp.mean(xc * xc, axis=-1, keepdims=True)
    ln = xc * lax.rsqrt(var + LN_EPS) * lng_ref[...] + lnb_ref[...]
    y_b = _dot((ln * jax.nn.sigmoid(ln)).astype(BF16), wb_ref[...])
    v_buf[0:HALO_CF, :] = v_buf[tm:tm + HALO_CF, :]

    pu = _dot(h, w_in_ref[:, COL_POOL:COL_POOL + W_BRANCH])
    p_buf[HALO_POOL:HALO_POOL + tm, :] = pu
    pos = (i * tm + lax.broadcasted_iota(jnp.int32, (tm, 1), 0)).astype(F32)
    win_sums = {}
    level, w = p_buf[...], 1
    for gi in range(len(POOL_WINDOWS)):
        level = level[SUBLANES:, :] + level[SUBLANES - w:level.shape[0] - w, :]
        w *= 2
        first_row = HALO_POOL - SUBLANES * (gi + 1)
        win_sums[w] = level[first_row:first_row + tm, 0:POOL_GROUP]
        if gi + 1 < len(POOL_WINDOWS):
            level = level[:, POOL_GROUP:]
    q_groups = []
    for gi, w in enumerate(POOL_WINDOWS):
        tok = pu[:, gi * POOL_GROUP:(gi + 1) * POOL_GROUP]
        inv_count = 1.0 / jnp.minimum(pos + 1.0, float(w))
        q_groups.append(_dot((win_sums[w] * inv_count - tok).astype(BF16), poolw_ref[gi]))
    q = jnp.concatenate(q_groups, axis=-1) * pscale_ref[...]
    y_c = _dot(q.astype(BF16), wc_ref[...])
    p_buf[0:HALO_POOL, :] = p_buf[tm:tm + HALO_POOL, :]

    gates = jax.nn.sigmoid(_dot(h, w_in_ref[:, COL_GATE:COL_GATE + 3 * D_MODEL]))
    merged = (gates[:, 0:D_MODEL] * y_a + gates[:, D_MODEL:2 * D_MODEL] * y_b
              + gates[:, 2 * D_MODEL:3 * D_MODEL] * y_c)
    out_ref[...] = x + _dot(merged.astype(BF16), wo_ref[...])


def _resident(shape):
    zeros = (0,) * len(shape)
    return pl.BlockSpec(shape, lambda b, i: zeros, pipeline_mode=pl.Buffered(1))


def _mixer(x, g, w_in, scw, cfw, cfb, lng, lnb, poolw, pscale, wa, wb, wc, wo):
    batch, seq, _ = x.shape
    tm = MIX_TM
    row = lambda v: v.reshape(1, -1)
    tile = pl.BlockSpec((None, tm, D_MODEL), lambda b, i: (b, i, 0))
    operands = (row(g), w_in.astype(BF16), scw, cfw, row(cfb), row(lng), row(lnb),
                poolw.astype(BF16), row(pscale), wa.astype(BF16), wb.astype(BF16),
                wc.astype(BF16), wo.astype(BF16))
    return pl.pallas_call(
        _mixer_kernel,
        grid=(batch, seq // tm),
        in_specs=[tile] + [_resident(op.shape) for op in operands],
        out_specs=tile,
        out_shape=jax.ShapeDtypeStruct(x.shape, F32),
        scratch_shapes=[pltpu.VMEM((HALO_SC + tm, W_BRANCH), F32),
                        pltpu.VMEM((HALO_CF + tm, W_BRANCH), F32),
                        pltpu.VMEM((HALO_POOL + tm, W_BRANCH), F32)],
        compiler_params=pltpu.CompilerParams(
            dimension_semantics=("arbitrary", "arbitrary"),
            vmem_limit_bytes=VMEM_LIMIT),
        name="mixer",
    )(x, *operands)


class _ExpertWeights:
    def __init__(self, w1_hbm, w3_hbm, w2_hbm, w1r, w3r, w2r, stage13, stage2, sem):
        self.hbm = (w1_hbm, w3_hbm, w2_hbm)
        self.w1r, self.w3r, self.w2r = w1r, w3r, w2r
        self.stage13, self.stage2, self.sem = stage13, stage2, sem
        self.n_steps, _, self.fc = w1r.shape

    def _copies(self, e, k):
        w1_hbm, w3_hbm, w2_hbm = self.hbm
        cols = slice(k * self.fc, (k + 1) * self.fc)
        return (pltpu.make_async_copy(w1_hbm.at[e, :, cols], self.stage13.at[0], self.sem.at[0]),
                pltpu.make_async_copy(w3_hbm.at[e, :, cols], self.stage13.at[1], self.sem.at[1]),
                pltpu.make_async_copy(w2_hbm.at[e, cols, :], self.stage2, self.sem.at[2]))

    def start_fetch(self, e, f):
        for k in range(self.n_steps):
            @pl.when(f == k)
            def _():
                for copy in self._copies(e, k):
                    copy.start()

    def finish_fetch(self, f):
        for copy in self._copies(0, 0):
            copy.wait()
        self.w1r[f] = self.stage13[0].astype(BF16)
        self.w3r[f] = self.stage13[1].astype(BF16)
        self.w2r[f] = self.stage2[...].astype(BF16)

    def swiglu(self, h, f):
        a = _dot(h, self.w1r[f])
        c = _dot(h, self.w3r[f])
        return _dot((a * jax.nn.sigmoid(a) * c).astype(BF16), self.w2r[f])


def _expert_weight_scratch(n_steps, fc):
    return [pltpu.VMEM((n_steps, D_MODEL, fc), BF16), pltpu.VMEM((n_steps, D_MODEL, fc), BF16),
            pltpu.VMEM((n_steps, fc, D_MODEL), BF16), pltpu.VMEM((2, D_MODEL, fc), F32),
            pltpu.VMEM((fc, D_MODEL), F32), pltpu.SemaphoreType.DMA((3,))]


def _dense_ffn_kernel(x_ref, g_ref, w1_hbm, w3_hbm, w2_hbm, out_ref, h_buf, acc_buf, *wbufs):
    i = pl.program_id(0)
    f = pl.program_id(1)
    weights = _ExpertWeights(w1_hbm, w3_hbm, w2_hbm, *wbufs)

    @pl.when(i == 0)
    def _():
        weights.start_fetch(0, f)
        weights.finish_fetch(f)

    @pl.when(f == 0)
    def _():
        x = x_ref[...]
        h_buf[...] = _rms_norm(x, g_ref[...]).astype(BF16)
        acc_buf[...] = x

    acc_buf[...] += weights.swiglu(h_buf[...], f)

    @pl.when(f == pl.num_programs(1) - 1)
    def _():
        out_ref[...] = acc_buf[...]


def _dense_ffn(x2d, g, w1, w3, w2):
    tokens = x2d.shape[0]
    tm, fc = FFN_TM, FFN_FC
    n_steps = D_FF // fc
    tile = pl.BlockSpec((tm, D_MODEL), lambda i, f: (i, 0))
    hbm = pl.BlockSpec(memory_space=pl.ANY)
    return pl.pallas_call(
        _dense_ffn_kernel,
        grid=(tokens // tm, n_steps),
        in_specs=[tile, pl.BlockSpec((1, D_MODEL), lambda i, f: (0, 0)), hbm, hbm, hbm],
        out_specs=tile,
        out_shape=jax.ShapeDtypeStruct(x2d.shape, F32),
        scratch_shapes=[pltpu.VMEM((tm, D_MODEL), BF16), pltpu.VMEM((tm, D_MODEL), F32)]
        + _expert_weight_scratch(n_steps, fc),
        compiler_params=pltpu.CompilerParams(
            dimension_semantics=("arbitrary", "arbitrary"),
            vmem_limit_bytes=VMEM_LIMIT),
        name="dense_ffn",
    )(x2d, g.reshape(1, -1), w1[None], w3[None], w2[None])


def _store_token_tiles(ref, value):
    rows = value.shape[0]
    for c in range(ROW_TILES):
        ref[pl.ds(c, rows, stride=ROW_TILES), :] = value[:, c * LANES:(c + 1) * LANES]


def _load_token_tiles(ref, rows):
    return [ref[pl.ds(c, rows, stride=ROW_TILES), :] for c in range(ROW_TILES)]


def _router_kernel(x_ref, g_ref, router_ref, ids_ref, probs_ref, ht_ref):
    h = _rms_norm(x_ref[...], g_ref[...])
    _store_token_tiles(ht_ref, h)
    logits = jnp.dot(h, router_ref[...], preferred_element_type=F32,
                     precision=lax.Precision.HIGHEST)
    lane = lax.broadcasted_iota(jnp.int32, logits.shape, 1)
    m1 = jnp.max(logits, axis=-1, keepdims=True)
    i1 = jnp.min(jnp.where(logits == m1, lane, N_EXPERTS), axis=-1, keepdims=True)
    rest = jnp.where(lane == i1, -jnp.inf, logits)
    m2 = jnp.max(rest, axis=-1, keepdims=True)
    i2 = jnp.min(jnp.where(rest == m2, lane, N_EXPERTS), axis=-1, keepdims=True)
    e2 = jnp.exp(m2 - m1)
    denom = 1.0 + e2
    first = lax.broadcasted_iota(jnp.int32, ids_ref.shape, 1) == 0
    ids_ref[...] = jnp.where(first, i1, i2)
    probs_ref[...] = jnp.where(first, 1.0 / denom, e2 / denom)


def _router(x2d, g, router):
    tokens = x2d.shape[0]
    tm = ROUTE_TM
    pair = pl.BlockSpec((tm, TOP_K), lambda i: (i, 0))
    return pl.pallas_call(
        _router_kernel,
        grid=(tokens // tm,),
        in_specs=[pl.BlockSpec((tm, D_MODEL), lambda i: (i, 0)),
                  pl.BlockSpec((1, D_MODEL), lambda i: (0, 0)),
                  pl.BlockSpec((D_MODEL, N_EXPERTS), lambda i: (0, 0))],
        out_specs=[pair, pair, pl.BlockSpec((tm * ROW_TILES, LANES), lambda i: (i, 0))],
        out_shape=[jax.ShapeDtypeStruct((tokens, TOP_K), jnp.int32),
                   jax.ShapeDtypeStruct((tokens, TOP_K), F32),
                   jax.ShapeDtypeStruct((tokens * ROW_TILES, LANES), F32)],
        compiler_params=pltpu.CompilerParams(dimension_semantics=("arbitrary",)),
        name="router",
    )(x2d, g.reshape(1, -1), router)


def _routing_tables(ids, tm):
    tokens = ids.shape[0]
    n = TOP_K * tokens
    n_pad = N_EXPERTS * tm
    n_tiles = (n + n_pad) // tm
    e_flat = ids.T.reshape(n)
    experts = jnp.arange(N_EXPERTS, dtype=jnp.int32)
    counts = jnp.sum((e_flat[:, None] == experts[None, :]).astype(jnp.int32), axis=0)
    tiles_per = (counts + tm - 1) // tm
    tile_end = jnp.cumsum(tiles_per)
    n_used = tile_end[-1]
    jt = jnp.arange(n_tiles, dtype=jnp.int32)
    tile_expert = jnp.minimum(
        jnp.sum((jt[:, None] >= tile_end[None, :]).astype(jnp.int32), axis=1), N_EXPERTS - 1)
    pad_end = jnp.cumsum(tiles_per * tm - counts)
    pad_key = jnp.sum((jnp.arange(n_pad, dtype=jnp.int32)[:, None] >= pad_end[None, :])
                      .astype(jnp.int32), axis=1)
    row_dst = jnp.argsort(jnp.concatenate([e_flat, pad_key]), stable=True).astype(jnp.int32)
    row_token = jnp.where(row_dst < n, row_dst % tokens, 0).astype(jnp.int32)
    return tile_expert.astype(jnp.int32), n_used.reshape(1).astype(jnp.int32), row_token, row_dst


def _moe_group_kernel(te_ref, nu_ref, tok_ref, dst_ref,
                      ht_hbm, w1_hbm, w3_hbm, w2_hbm, y_hbm,
                      xbuf, h_buf, acc_buf, obuf, gather_sem, scatter_sem, *wbufs):
    weights = _ExpertWeights(w1_hbm, w3_hbm, w2_hbm, *wbufs)
    n_steps = weights.n_steps
    j = pl.program_id(0)
    f = pl.program_id(1)
    n_used = nu_ref[0]
    tm = h_buf.shape[0]
    chunk = tm // n_steps
    tail_rows = range(n_steps * chunk, tm)
    used = j < n_used
    has_prev = j > 0
    has_next = j + 1 < n_used
    slot = lax.rem(j, 2)

    def row_tile(r):
        return pl.ds(pl.multiple_of(r * ROW_TILES, ROW_TILES), ROW_TILES)

    def gather_row(tile, r):
        pltpu.make_async_copy(ht_hbm.at[row_tile(tok_ref[tile * tm + r]), :],
                              xbuf.at[row_tile(r), :], gather_sem).start()

    def scatter_row(tile, r):
        s = lax.rem(tile, 2)
        pltpu.make_async_copy(obuf.at[s].at[row_tile(r), :],
                              y_hbm.at[row_tile(dst_ref[tile * tm + r]), :],
                              scatter_sem.at[s]).start()

    def wait_gather():
        pltpu.make_async_copy(ht_hbm.at[pl.ds(0, tm * ROW_TILES), :], xbuf, gather_sem).wait()

    def wait_scatter(s):
        pltpu.make_async_copy(obuf.at[s], y_hbm.at[pl.ds(0, tm * ROW_TILES), :],
                              scatter_sem.at[s]).wait()

    def for_all_rows(row_fn, tile):
        def body(r, carry):
            row_fn(tile, r)
            return carry
        lax.fori_loop(0, tm, body, 0)

    @pl.when(jnp.logical_and(f == 0, j == 0))
    def _():
        for_all_rows(gather_row, 0)

    @pl.when(jnp.logical_and(f == 0, used))
    def _():
        wait_gather()
        for c, part in enumerate(_load_token_tiles(xbuf, tm)):
            h_buf[:, c * LANES:(c + 1) * LANES] = part.astype(BF16)
        acc_buf[...] = jnp.zeros_like(acc_buf)

    @pl.when(jnp.logical_and(f == 0, has_next))
    def _():
        for r in tail_rows:
            gather_row(j + 1, r)

    @pl.when(jnp.logical_and(f == 0, jnp.logical_and(used, has_prev)))
    def _():
        for r in tail_rows:
            scatter_row(j - 1, r)

    next_expert = te_ref[jnp.minimum(j + 1, pl.num_programs(0) - 1)]
    expert_ends = jnp.logical_and(has_next, next_expert != te_ref[j])

    @pl.when(j == 0)
    def _():
        weights.start_fetch(te_ref[0], f)
        weights.finish_fetch(f)

    @pl.when(expert_ends)
    def _():
        weights.start_fetch(next_expert, f)

    def step(do_gather, do_scatter):
        base = f * chunk
        for k in range(chunk):
            if do_gather:
                gather_row(j + 1, base + k)
            if do_scatter:
                scatter_row(j - 1, base + k)
        acc_buf[...] += weights.swiglu(h_buf[...], f)

    @pl.when(jnp.logical_and(has_prev, has_next))
    def _():
        step(True, True)

    @pl.when(jnp.logical_and(jnp.logical_not(has_prev), has_next))
    def _():
        step(True, False)

    @pl.when(jnp.logical_and(used, jnp.logical_and(has_prev, jnp.logical_not(has_next))))
    def _():
        step(False, True)

    @pl.when(jnp.logical_and(used, jnp.logical_not(jnp.logical_or(has_prev, has_next))))
    def _():
        step(False, False)

    @pl.when(expert_ends)
    def _():
        weights.finish_fetch(f)

    @pl.when(jnp.logical_and(f == n_steps - 1, used))
    def _():
        @pl.when(j >= 2)
        def _():
            wait_scatter(slot)
        _store_token_tiles(obuf.at[slot], acc_buf[...])

        @pl.when(jnp.logical_not(has_next))
        def _():
            for_all_rows(scatter_row, j)
            wait_scatter(slot)

            @pl.when(has_prev)
            def _():
                wait_scatter(1 - slot)

    @pl.when(jnp.logical_and(f == 0, jnp.logical_not(used)))
    def _():
        obuf[0] = jnp.zeros(obuf.shape[1:], F32)
        rows = tm * ROW_TILES
        fill = pltpu.make_async_copy(obuf.at[0],
                                     y_hbm.at[pl.ds(pl.multiple_of(j * rows, rows), rows), :],
                                     scatter_sem.at[0])
        fill.start()
        fill.wait()


def _moe_grouped(ht, w1, w3, w2, tables):
    tile_expert, n_used, row_token, row_dst = tables
    tm, fc = MOE_TM, FFN_FC
    n_rows = row_token.shape[0]
    tile_rows = tm * ROW_TILES
    n_steps = D_FF // fc
    hbm = pl.BlockSpec(memory_space=pl.ANY)
    grid_spec = pltpu.PrefetchScalarGridSpec(
        num_scalar_prefetch=4,
        grid=(n_rows // tm, n_steps),
        in_specs=[hbm, hbm, hbm, hbm],
        out_specs=hbm,
        scratch_shapes=[pltpu.VMEM((tile_rows, LANES), F32), pltpu.VMEM((tm, D_MODEL), BF16),
                        pltpu.VMEM((tm, D_MODEL), F32), pltpu.VMEM((2, tile_rows, LANES), F32),
                        pltpu.SemaphoreType.DMA(()), pltpu.SemaphoreType.DMA((2,))]
        + _expert_weight_scratch(n_steps, fc),
    )
    return pl.pallas_call(
        _moe_group_kernel,
        grid_spec=grid_spec,
        out_shape=jax.ShapeDtypeStruct((n_rows * ROW_TILES, LANES), F32),
        compiler_params=pltpu.CompilerParams(
            dimension_semantics=("arbitrary", "arbitrary"),
            vmem_limit_bytes=VMEM_LIMIT),
        name="moe_grouped",
    )(tile_expert, n_used, row_token, row_dst, ht, w1, w3, w2)


def _combine_kernel(x_ref, y0_ref, y1_ref, p_ref, gf_ref, out_ref, *, final_norm):
    tm = x_ref.shape[0]
    p = p_ref[...]
    y0 = _load_token_tiles(y0_ref, tm)
    y1 = _load_token_tiles(y1_ref, tm)
    y = jnp.concatenate(
        [x_ref[:, c * LANES:(c + 1) * LANES] + p[:, 0:1] * y0[c] + p[:, 1:2] * y1[c]
         for c in range(ROW_TILES)], axis=-1)
    out_ref[...] = _rms_norm(y, gf_ref[...]) if final_norm else y


def _combine(x2d, y_tiles, probs, g_final, *, final_norm):
    tokens = x2d.shape[0]
    tm = COMBINE_TM
    tile = pl.BlockSpec((tm, D_MODEL), lambda i: (i, 0))
    first = pl.BlockSpec((tm * ROW_TILES, LANES), lambda i: (i, 0))
    second = pl.BlockSpec((tm * ROW_TILES, LANES), lambda i: (i + tokens // tm, 0))
    return pl.pallas_call(
        functools.partial(_combine_kernel, final_norm=final_norm),
        grid=(tokens // tm,),
        in_specs=[tile, first, second,
                  pl.BlockSpec((tm, TOP_K), lambda i: (i, 0)),
                  pl.BlockSpec((1, D_MODEL), lambda i: (0, 0))],
        out_specs=tile,
        out_shape=jax.ShapeDtypeStruct(x2d.shape, F32),
        compiler_params=pltpu.CompilerParams(dimension_semantics=("arbitrary",)),
        name="moe_combine",
    )(x2d, y_tiles, y_tiles, probs, g_final.reshape(1, -1))


def _moe_ffn(x2d, g, g_final, router, w1, w3, w2, *, final_norm):
    ids, probs, ht = _router(x2d, g, router)
    tables = _routing_tables(ids, MOE_TM)
    y_tiles = _moe_grouped(ht, w1, w3, w2, tables)
    return _combine(x2d, y_tiles, probs, g_final, final_norm=final_norm)


def kernel(x, norm_mix_g, w_in, sc_conv_w, cf_conv_w, cf_conv_b, cf_ln_g, cf_ln_b,
           pool_w, pool_scale, w_sc_out, w_cf_out, w_pool_out, w_o, norm_ffn_g,
           dense_w1, dense_w3, dense_w2, moe_router, moe_w1, moe_w3, moe_w2,
           norm_final_g):
    batch, seq, d = x.shape
    depth = w_in.shape[0]
    for layer in range(depth):
        x = _mixer(x, norm_mix_g[layer], w_in[layer], sc_conv_w[layer], cf_conv_w[layer],
                   cf_conv_b[layer], cf_ln_g[layer], cf_ln_b[layer], pool_w[layer],
                   pool_scale[layer], w_sc_out[layer], w_cf_out[layer], w_pool_out[layer],
                   w_o[layer])
        x2d = x.reshape(batch * seq, d)
        i = layer // 2
        if layer % 2 == 0:
            x2d = _dense_ffn(x2d, norm_ffn_g[layer], dense_w1[i], dense_w3[i], dense_w2[i])
        else:
            x2d = _moe_ffn(x2d, norm_ffn_g[layer], norm_final_g, moe_router[i],
                           moe_w1[i], moe_w3[i], moe_w2[i],
                           final_norm=(layer == depth - 1))
        x = x2d.reshape(batch, seq, d)
    if depth % 2 == 1:
        x = _final_norm(x, norm_final_g)
    return x


def _final_norm_kernel(x_ref, g_ref, out_ref):
    out_ref[...] = _rms_norm(x_ref[...], g_ref[...])


def _final_norm(x, g):
    batch, seq, d = x.shape
    x2d = x.reshape(batch * seq, d)
    tile = pl.BlockSpec((FFN_TM, d), lambda i: (i, 0))
    out = pl.pallas_call(
        _final_norm_kernel,
        grid=(x2d.shape[0] // FFN_TM,),
        in_specs=[tile, pl.BlockSpec((1, d), lambda i: (0, 0))],
        out_specs=tile,
        out_shape=jax.ShapeDtypeStruct(x2d.shape, F32),
        name="final_norm",
    )(x2d, g.reshape(1, -1))
    return out.reshape(batch, seq, d)
```

```python
import functools

import jax
import jax.numpy as jnp
from jax import lax
from jax.experimental import pallas as pl
from jax.experimental.pallas import tpu as pltpu

D_MODEL = 1024
W_BRANCH = 1024
SC_KERNEL = 3
CF_KERNEL = 31
POOL_WINDOWS = (2, 4, 8, 16)
POOL_GROUP = W_BRANCH // len(POOL_WINDOWS)
D_FF = 3584
N_EXPERTS = 8
TOP_K = 2
RMS_EPS = 1e-6
LN_EPS = 1e-5

COL_SC = 0
COL_CF = 3 * W_BRANCH
COL_POOL = 5 * W_BRANCH
COL_GATE = 6 * W_BRANCH
IN_COLS = 9 * W_BRANCH

SUBLANES = 8
LANES = 128
ROW_TILES = D_MODEL // LANES
assert ROW_TILES == SUBLANES
assert POOL_WINDOWS == tuple(2 ** (g + 1) for g in range(len(POOL_WINDOWS)))

HALO_SC = SUBLANES
HALO_CF = SUBLANES * (-(-CF_KERNEL // SUBLANES))
HALO_POOL = SUBLANES * len(POOL_WINDOWS)

MIX_TM = 256
CONV_ROWS = 64
FFN_TM = 512
FFN_FC = 512
MOE_TM = 512
ROUTE_TM = 512
COMBINE_TM = 512
VMEM_LIMIT = 60 * 1024 * 1024

F32 = jnp.float32
BF16 = jnp.bfloat16


def _rms_norm(x, g):
    ms = jnp.mean(x * x, axis=-1, keepdims=True)
    return x * lax.rsqrt(ms + RMS_EPS) * g


def _dot(a, b):
    return jnp.dot(a, b, preferred_element_type=F32)


def _mixer_kernel(x_ref, g_ref, w_sc_ref, w_cf_ref, w_pool_ref, w_gate_ref, scw_ref, cfw_ref, cfb_ref, lng_ref, lnb_ref,
                  poolw_ref, pscale_ref, wa_ref, wb_ref, wc_ref, wo_ref,
                  out_ref, u_buf, v_buf, p_buf, conv_buf, h_buf):
    i = pl.program_id(1)
    tm = x_ref.shape[0]

    @pl.when(i == 0)
    def _():
        u_buf[0:HALO_SC, :] = jnp.zeros((HALO_SC, W_BRANCH), F32)
        v_buf[0:HALO_CF, :] = jnp.zeros((HALO_CF, W_BRANCH), F32)
        p_buf[0:HALO_POOL, :] = jnp.zeros((HALO_POOL, W_BRANCH), F32)

    x = x_ref[...]
    h_buf[...] = _rms_norm(x, g_ref[...]).astype(BF16)
    h = h_buf[...]

    pb = _dot(h, w_cf_ref[...])
    v = pb[:, 0:W_BRANCH] * jax.nn.sigmoid(pb[:, W_BRANCH:2 * W_BRANCH])
    v_buf[HALO_CF:HALO_CF + tm, :] = v
    pa = _dot(h, w_sc_ref[...])
    pu = _dot(h, w_pool_ref[...])
    gate_logits = _dot(h, w_gate_ref[...])

    ext = CONV_ROWS + SUBLANES
    for row0 in range(0, tm, CONV_ROWS):
        for c in range(W_BRANCH // LANES):
            lanes = slice(c * LANES, (c + 1) * LANES)
            total = None
            for r in range(SUBLANES):
                part = None
                for q in range(-(-CF_KERNEL // SUBLANES)):
                    d = SUBLANES * q + r
                    if d < CF_KERNEL:
                        base = row0 + HALO_CF - SUBLANES * (q + 1)
                        term = (cfw_ref[CF_KERNEL - 1 - d:CF_KERNEL - d, lanes]
                                * v_buf[base:base + ext, lanes])
                        part = term if part is None else part + term
                part = part[SUBLANES - r:SUBLANES - r + CONV_ROWS, :]
                total = part if total is None else total + part
            conv_buf[row0:row0 + CONV_ROWS, lanes] = total + cfb_ref[:, lanes]
    acc = conv_buf[...]
    mu = jnp.mean(acc, axis=-1, keepdims=True)
    xc = acc - mu
    var = jnp.mean(xc * xc, axis=-1, keepdims=True)
    ln = xc * lax.rsqrt(var + LN_EPS) * lng_ref[...] + lnb_ref[...]
    y_b = _dot((ln * jax.nn.sigmoid(ln)).astype(BF16), wb_ref[...])
    v_buf[0:HALO_CF, :] = v_buf[tm:tm + HALO_CF, :]

    u = pa[:, W_BRANCH:2 * W_BRANCH] * pa[:, 2 * W_BRANCH:3 * W_BRANCH]
    u_buf[HALO_SC:HALO_SC + tm, :] = u
    conv = scw_ref[SC_KERNEL - 1:SC_KERNEL, :] * u
    for k in range(SC_KERNEL - 1):
        off = HALO_SC - (SC_KERNEL - 1) + k
        conv = conv + scw_ref[k:k + 1, :] * u_buf[off:off + tm, :]
    y_a = _dot((pa[:, 0:W_BRANCH] * conv).astype(BF16), wa_ref[...])
    u_buf[0:HALO_SC, :] = u_buf[tm:tm + HALO_SC, :]

    p_buf[HALO_POOL:HALO_POOL + tm, :] = pu
    pos = (i * tm + lax.broadcasted_iota(jnp.int32, (tm, 1), 0)).astype(F32)
    win_sums = {}
    level, w = p_buf[...], 1
    for gi in range(len(POOL_WINDOWS)):
        level = level[SUBLANES:, :] + level[SUBLANES - w:level.shape[0] - w, :]
        w *= 2
        first_row = HALO_POOL - SUBLANES * (gi + 1)
        win_sums[w] = level[first_row:first_row + tm, 0:POOL_GROUP]
        if gi + 1 < len(POOL_WINDOWS):
            level = level[:, POOL_GROUP:]
    q_groups = []
    for gi, w in enumerate(POOL_WINDOWS):
        tok = pu[:, gi * POOL_GROUP:(gi + 1) * POOL_GROUP]
        inv_count = 1.0 / jnp.minimum(pos + 1.0, float(w))
        q_groups.append(_dot((win_sums[w] * inv_count - tok).astype(BF16), poolw_ref[gi]))
    q = jnp.concatenate(q_groups, axis=-1) * pscale_ref[...]
    y_c = _dot(q.astype(BF16), wc_ref[...])
    p_buf[0:HALO_POOL, :] = p_buf[tm:tm + HALO_POOL, :]

    gates = jax.nn.sigmoid(gate_logits)
    merged = (gates[:, 0:D_MODEL] * y_a + gates[:, D_MODEL:2 * D_MODEL] * y_b
              + gates[:, 2 * D_MODEL:3 * D_MODEL] * y_c)
    out_ref[...] = x + _dot(merged.astype(BF16), wo_ref[...])


def _resident(shape):
    zeros = (0,) * len(shape)
    return pl.BlockSpec(shape, lambda b, i: zeros, pipeline_mode=pl.Buffered(1))


def _mixer(x, g, w_in, scw, cfw, cfb, lng, lnb, poolw, pscale, wa, wb, wc, wo):
    batch, seq, _ = x.shape
    tm = MIX_TM
    row = lambda v: v.reshape(1, -1)
    tile = pl.BlockSpec((None, tm, D_MODEL), lambda b, i: (b, i, 0))
    w_sc, w_cf, w_pool, w_gate = (
        w_in[:, lo:hi].astype(BF16) for lo, hi in
        ((COL_SC, COL_CF), (COL_CF, COL_POOL), (COL_POOL, COL_GATE), (COL_GATE, IN_COLS)))
    operands = (row(g), w_sc, w_cf, w_pool, w_gate, scw, cfw, row(cfb), row(lng), row(lnb),
                poolw.astype(BF16), row(pscale), wa.astype(BF16), wb.astype(BF16),
                wc.astype(BF16), wo.astype(BF16))
    return pl.pallas_call(
        _mixer_kernel,
        grid=(batch, seq // tm),
        in_specs=[tile] + [_resident(op.shape) for op in operands],
        out_specs=tile,
        out_shape=jax.ShapeDtypeStruct(x.shape, F32),
        scratch_shapes=[pltpu.VMEM((HALO_SC + tm, W_BRANCH), F32),
                        pltpu.VMEM((HALO_CF + tm, W_BRANCH), F32),
                        pltpu.VMEM((HALO_POOL + tm, W_BRANCH), F32),
                        pltpu.VMEM((tm, W_BRANCH), F32),
                        pltpu.VMEM((tm, D_MODEL), BF16)],
        compiler_params=pltpu.CompilerParams(
            dimension_semantics=("arbitrary", "arbitrary"),
            vmem_limit_bytes=VMEM_LIMIT),
        name="mixer",
    )(x, *operands)


class _ExpertWeights:
    def __init__(self, w1_hbm, w3_hbm, w2_hbm, w1r, w3r, w2r, stage13, stage2, sem):
        self.hbm = (w1_hbm, w3_hbm, w2_hbm)
        self.w1r, self.w3r, self.w2r = w1r, w3r, w2r
        self.stage13, self.stage2, self.sem = stage13, stage2, sem
        self.n_steps, _, self.fc = w1r.shape

    def _copies(self, e, k):
        w1_hbm, w3_hbm, w2_hbm = self.hbm
        cols = slice(k * self.fc, (k + 1) * self.fc)
        return (pltpu.make_async_copy(w1_hbm.at[e, :, cols], self.stage13.at[0], self.sem.at[0]),
                pltpu.make_async_copy(w3_hbm.at[e, :, cols], self.stage13.at[1], self.sem.at[1]),
                pltpu.make_async_copy(w2_hbm.at[e, cols, :], self.stage2, self.sem.at[2]))

    def start_fetch(self, e, f):
        for k in range(self.n_steps):
            @pl.when(f == k)
            def _():
                for copy in self._copies(e, k):
                    copy.start()

    def finish_fetch(self, f):
        for copy in self._copies(0, 0):
            copy.wait()
        self.w1r[f] = self.stage13[0].astype(BF16)
        self.w3r[f] = self.stage13[1].astype(BF16)
        self.w2r[f] = self.stage2[...].astype(BF16)

    def swiglu(self, h, f):
        a = _dot(h, self.w1r[f])
        c = _dot(h, self.w3r[f])
        return _dot((a * jax.nn.sigmoid(a) * c).astype(BF16), self.w2r[f])


def _expert_weight_scratch(n_steps, fc):
    return [pltpu.VMEM((n_steps, D_MODEL, fc), BF16), pltpu.VMEM((n_steps, D_MODEL, fc), BF16),
            pltpu.VMEM((n_steps, fc, D_MODEL), BF16), pltpu.VMEM((2, D_MODEL, fc), F32),
            pltpu.VMEM((fc, D_MODEL), F32), pltpu.SemaphoreType.DMA((3,))]


def _dense_ffn_kernel(x_ref, g_ref, w1_hbm, w3_hbm, w2_hbm, out_ref, h_buf, acc_buf, *wbufs):
    i = pl.program_id(0)
    f = pl.program_id(1)
    weights = _ExpertWeights(w1_hbm, w3_hbm, w2_hbm, *wbufs)

    @pl.when(i == 0)
    def _():
        weights.start_fetch(0, f)
        weights.finish_fetch(f)

    @pl.when(f == 0)
    def _():
        x = x_ref[...]
        h_buf[...] = _rms_norm(x, g_ref[...]).astype(BF16)
        acc_buf[...] = x

    acc_buf[...] += weights.swiglu(h_buf[...], f)

    @pl.when(f == pl.num_programs(1) - 1)
    def _():
        out_ref[...] = acc_buf[...]


def _dense_ffn(x2d, g, w1, w3, w2):
    tokens = x2d.shape[0]
    tm, fc = FFN_TM, FFN_FC
    n_steps = D_FF // fc
    tile = pl.BlockSpec((tm, D_MODEL), lambda i, f: (i, 0))
    hbm = pl.BlockSpec(memory_space=pl.ANY)
    return pl.pallas_call(
        _dense_ffn_kernel,
        grid=(tokens // tm, n_steps),
        in_specs=[tile, pl.BlockSpec((1, D_MODEL), lambda i, f: (0, 0)), hbm, hbm, hbm],
        out_specs=tile,
        out_shape=jax.ShapeDtypeStruct(x2d.shape, F32),
        scratch_shapes=[pltpu.VMEM((tm, D_MODEL), BF16), pltpu.VMEM((tm, D_MODEL), F32)]
        + _expert_weight_scratch(n_steps, fc),
        compiler_params=pltpu.CompilerParams(
            dimension_semantics=("arbitrary", "arbitrary"),
            vmem_limit_bytes=VMEM_LIMIT),
        name="dense_ffn",
    )(x2d, g.reshape(1, -1), w1[None], w3[None], w2[None])


def _store_token_tiles(ref, value):
    rows = value.shape[0]
    for c in range(ROW_TILES):
        ref[pl.ds(c, rows, stride=ROW_TILES), :] = value[:, c * LANES:(c + 1) * LANES]


def _load_token_tiles(ref, rows):
    return [ref[pl.ds(c, rows, stride=ROW_TILES), :] for c in range(ROW_TILES)]


def _router_kernel(x_ref, g_ref, router_ref, ids_ref, probs_ref, ht_ref):
    h = _rms_norm(x_ref[...], g_ref[...])
    _store_token_tiles(ht_ref, h)
    logits = jnp.dot(h, router_ref[...], preferred_element_type=F32,
                     precision=lax.Precision.HIGHEST)
    lane = lax.broadcasted_iota(jnp.int32, logits.shape, 1)
    m1 = jnp.max(logits, axis=-1, keepdims=True)
    i1 = jnp.min(jnp.where(logits == m1, lane, N_EXPERTS), axis=-1, keepdims=True)
    rest = jnp.where(lane == i1, -jnp.inf, logits)
    m2 = jnp.max(rest, axis=-1, keepdims=True)
    i2 = jnp.min(jnp.where(rest == m2, lane, N_EXPERTS), axis=-1, keepdims=True)
    e2 = jnp.exp(m2 - m1)
    denom = 1.0 + e2
    first = lax.broadcasted_iota(jnp.int32, ids_ref.shape, 1) == 0
    ids_ref[...] = jnp.where(first, i1, i2)
    probs_ref[...] = jnp.where(first, 1.0 / denom, e2 / denom)


def _router(x2d, g, router):
    tokens = x2d.shape[0]
    tm = ROUTE_TM
    pair = pl.BlockSpec((tm, TOP_K), lambda i: (i, 0))
    return pl.pallas_call(
        _router_kernel,
        grid=(tokens // tm,),
        in_specs=[pl.BlockSpec((tm, D_MODEL), lambda i: (i, 0)),
                  pl.BlockSpec((1, D_MODEL), lambda i: (0, 0)),
                  pl.BlockSpec((D_MODEL, N_EXPERTS), lambda i: (0, 0))],
        out_specs=[pair, pair, pl.BlockSpec((tm * ROW_TILES, LANES), lambda i: (i, 0))],
        out_shape=[jax.ShapeDtypeStruct((tokens, TOP_K), jnp.int32),
                   jax.ShapeDtypeStruct((tokens, TOP_K), F32),
                   jax.ShapeDtypeStruct((tokens * ROW_TILES, LANES), F32)],
        compiler_params=pltpu.CompilerParams(dimension_semantics=("arbitrary",)),
        name="router",
    )(x2d, g.reshape(1, -1), router)


def _routing_tables(ids, tm):
    tokens = ids.shape[0]
    n = TOP_K * tokens
    n_pad = N_EXPERTS * tm
    n_tiles = (n + n_pad) // tm
    e_flat = ids.T.reshape(n)
    experts = jnp.arange(N_EXPERTS, dtype=jnp.int32)
    counts = jnp.sum((e_flat[:, None] == experts[None, :]).astype(jnp.int32), axis=0)
    tiles_per = (counts + tm - 1) // tm
    tile_end = jnp.cumsum(tiles_per)
    n_used = tile_end[-1]
    jt = jnp.arange(n_tiles, dtype=jnp.int32)
    tile_expert = jnp.minimum(
        jnp.sum((jt[:, None] >= tile_end[None, :]).astype(jnp.int32), axis=1), N_EXPERTS - 1)
    pad_end = jnp.cumsum(tiles_per * tm - counts)
    pad_key = jnp.sum((jnp.arange(n_pad, dtype=jnp.int32)[:, None] >= pad_end[None, :])
                      .astype(jnp.int32), axis=1)
    row_dst = jnp.argsort(jnp.concatenate([e_flat, pad_key]), stable=True).astype(jnp.int32)
    row_token = jnp.where(row_dst < n, row_dst % tokens, 0).astype(jnp.int32)
    return tile_expert.astype(jnp.int32), n_used.reshape(1).astype(jnp.int32), row_token, row_dst


def _moe_group_kernel(te_ref, nu_ref, tok_ref, dst_ref,
                      ht_hbm, w1_hbm, w3_hbm, w2_hbm, y_hbm,
                      xbuf, h_buf, acc_buf, obuf, gather_sem, scatter_sem, *wbufs):
    weights = _ExpertWeights(w1_hbm, w3_hbm, w2_hbm, *wbufs)
    n_steps = weights.n_steps
    j = pl.program_id(0)
    f = pl.program_id(1)
    n_used = nu_ref[0]
    tm = h_buf.shape[0]
    chunk = tm // n_steps
    tail_rows = range(n_steps * chunk, tm)
    used = j < n_used
    has_prev = j > 0
    has_next = j + 1 < n_used
    slot = lax.rem(j, 2)

    def row_tile(r):
        return pl.ds(pl.multiple_of(r * ROW_TILES, ROW_TILES), ROW_TILES)

    def gather_row(tile, r, queue=0):
        pltpu.make_async_copy(ht_hbm.at[row_tile(tok_ref[tile * tm + r]), :],
                              xbuf.at[row_tile(r), :], gather_sem).start(priority=queue)

    def scatter_row(tile, r, queue=0):
        s = lax.rem(tile, 2)
        pltpu.make_async_copy(obuf.at[s].at[row_tile(r), :],
                              y_hbm.at[row_tile(dst_ref[tile * tm + r]), :],
                              scatter_sem.at[s]).start(priority=queue)

    def wait_gather():
        pltpu.make_async_copy(ht_hbm.at[pl.ds(0, tm * ROW_TILES), :], xbuf, gather_sem).wait()

    def wait_scatter(s):
        pltpu.make_async_copy(obuf.at[s], y_hbm.at[pl.ds(0, tm * ROW_TILES), :],
                              scatter_sem.at[s]).wait()

    def for_all_rows(row_fn, tile):
        def body(r, carry):
            row_fn(tile, r)
            return carry
        lax.fori_loop(0, tm, body, 0)

    @pl.when(jnp.logical_and(f == 0, j == 0))
    def _():
        for_all_rows(gather_row, 0)

    @pl.when(jnp.logical_and(f == 0, used))
    def _():
        wait_gather()
        for c, part in enumerate(_load_token_tiles(xbuf, tm)):
            h_buf[:, c * LANES:(c + 1) * LANES] = part.astype(BF16)
        acc_buf[...] = jnp.zeros_like(acc_buf)

    @pl.when(jnp.logical_and(f == 0, has_next))
    def _():
        for r in tail_rows:
            gather_row(j + 1, r)

    @pl.when(jnp.logical_and(f == 0, jnp.logical_and(used, has_prev)))
    def _():
        for r in tail_rows:
            scatter_row(j - 1, r)

    next_expert = te_ref[jnp.minimum(j + 1, pl.num_programs(0) - 1)]
    expert_ends = jnp.logical_and(has_next, next_expert != te_ref[j])

    @pl.when(j == 0)
    def _():
        weights.start_fetch(te_ref[0], f)
        weights.finish_fetch(f)

    @pl.when(expert_ends)
    def _():
        weights.start_fetch(next_expert, f)

    def step(do_gather, do_scatter):
        base = f * chunk
        for k in range(chunk):
            if do_gather:
                gather_row(j + 1, base + k, queue=k % 2)
            if do_scatter:
                scatter_row(j - 1, base + k, queue=k % 2)
        acc_buf[...] += weights.swiglu(h_buf[...], f)

    @pl.when(jnp.logical_and(has_prev, has_next))
    def _():
        step(True, True)

    @pl.when(jnp.logical_and(jnp.logical_not(has_prev), has_next))
    def _():
        step(True, False)

    @pl.when(jnp.logical_and(used, jnp.logical_and(has_prev, jnp.logical_not(has_next))))
    def _():
        step(False, True)

    @pl.when(jnp.logical_and(used, jnp.logical_not(jnp.logical_or(has_prev, has_next))))
    def _():
        step(False, False)

    @pl.when(expert_ends)
    def _():
        weights.finish_fetch(f)

    @pl.when(jnp.logical_and(f == n_steps - 1, used))
    def _():
        @pl.when(j >= 2)
        def _():
            wait_scatter(slot)
        _store_token_tiles(obuf.at[slot], acc_buf[...])

        @pl.when(jnp.logical_not(has_next))
        def _():
            for_all_rows(scatter_row, j)
            wait_scatter(slot)

            @pl.when(has_prev)
            def _():
                wait_scatter(1 - slot)

    @pl.when(jnp.logical_and(f == 0, jnp.logical_not(used)))
    def _():
        obuf[0] = jnp.zeros(obuf.shape[1:], F32)
        rows = tm * ROW_TILES
        fill = pltpu.make_async_copy(obuf.at[0],
                                     y_hbm.at[pl.ds(pl.multiple_of(j * rows, rows), rows), :],
                                     scatter_sem.at[0])
        fill.start()
        fill.wait()


def _moe_grouped(ht, w1, w3, w2, tables):
    tile_expert, n_used, row_token, row_dst = tables
    tm, fc = MOE_TM, FFN_FC
    n_rows = row_token.shape[0]
    tile_rows = tm * ROW_TILES
    n_steps = D_FF // fc
    hbm = pl.BlockSpec(memory_space=pl.ANY)
    grid_spec = pltpu.PrefetchScalarGridSpec(
        num_scalar_prefetch=4,
        grid=(n_rows // tm, n_steps),
        in_specs=[hbm, hbm, hbm, hbm],
        out_specs=hbm,
        scratch_shapes=[pltpu.VMEM((tile_rows, LANES), F32), pltpu.VMEM((tm, D_MODEL), BF16),
                        pltpu.VMEM((tm, D_MODEL), F32), pltpu.VMEM((2, tile_rows, LANES), F32),
                        pltpu.SemaphoreType.DMA(()), pltpu.SemaphoreType.DMA((2,))]
        + _expert_weight_scratch(n_steps, fc),
    )
    return pl.pallas_call(
        _moe_group_kernel,
        grid_spec=grid_spec,
        out_shape=jax.ShapeDtypeStruct((n_rows * ROW_TILES, LANES), F32),
        compiler_params=pltpu.CompilerParams(
            dimension_semantics=("arbitrary", "arbitrary"),
            vmem_limit_bytes=VMEM_LIMIT),
        name="moe_grouped",
    )(tile_expert, n_used, row_token, row_dst, ht, w1, w3, w2)


def _combine_kernel(x_ref, y0_ref, y1_ref, p_ref, gf_ref, out_ref, *, final_norm):
    tm = x_ref.shape[0]
    p = p_ref[...]
    y0 = _load_token_tiles(y0_ref, tm)
    y1 = _load_token_tiles(y1_ref, tm)
    y = jnp.concatenate(
        [x_ref[:, c * LANES:(c + 1) * LANES] + p[:, 0:1] * y0[c] + p[:, 1:2] * y1[c]
         for c in range(ROW_TILES)], axis=-1)
    out_ref[...] = _rms_norm(y, gf_ref[...]) if final_norm else y


def _combine(x2d, y_tiles, probs, g_final, *, final_norm):
    tokens = x2d.shape[0]
    tm = COMBINE_TM
    tile = pl.BlockSpec((tm, D_MODEL), lambda i: (i, 0))
    first = pl.BlockSpec((tm * ROW_TILES, LANES), lambda i: (i, 0))
    second = pl.BlockSpec((tm * ROW_TILES, LANES), lambda i: (i + tokens // tm, 0))
    return pl.pallas_call(
        functools.partial(_combine_kernel, final_norm=final_norm),
        grid=(tokens // tm,),
        in_specs=[tile, first, second,
                  pl.BlockSpec((tm, TOP_K), lambda i: (i, 0)),
                  pl.BlockSpec((1, D_MODEL), lambda i: (0, 0))],
        out_specs=tile,
        out_shape=jax.ShapeDtypeStruct(x2d.shape, F32),
        compiler_params=pltpu.CompilerParams(dimension_semantics=("arbitrary",)),
        name="moe_combine",
    )(x2d, y_tiles, y_tiles, probs, g_final.reshape(1, -1))


def _moe_ffn(x2d, g, g_final, router, w1, w3, w2, *, final_norm):
    ids, probs, ht = _router(x2d, g, router)
    tables = _routing_tables(ids, MOE_TM)
    y_tiles = _moe_grouped(ht, w1, w3, w2, tables)
    return _combine(x2d, y_tiles, probs, g_final, final_norm=final_norm)


def kernel(x, norm_mix_g, w_in, sc_conv_w, cf_conv_w, cf_conv_b, cf_ln_g, cf_ln_b,
           pool_w, pool_scale, w_sc_out, w_cf_out, w_pool_out, w_o, norm_ffn_g,
           dense_w1, dense_w3, dense_w2, moe_router, moe_w1, moe_w3, moe_w2,
           norm_final_g):
    batch, seq, d = x.shape
    depth = w_in.shape[0]
    for layer in range(depth):
        x = _mixer(x, norm_mix_g[layer], w_in[layer], sc_conv_w[layer], cf_conv_w[layer],
                   cf_conv_b[layer], cf_ln_g[layer], cf_ln_b[layer], pool_w[layer],
                   pool_scale[layer], w_sc_out[layer], w_cf_out[layer], w_pool_out[layer],
                   w_o[layer])
        x2d = x.reshape(batch * seq, d)
        i = layer // 2
        if layer % 2 == 0:
            x2d = _dense_ffn(x2d, norm_ffn_g[layer], dense_w1[i], dense_w3[i], dense_w2[i])
        else:
            x2d = _moe_ffn(x2d, norm_ffn_g[layer], norm_final_g, moe_router[i],
                           moe_w1[i], moe_w3[i], moe_w2[i],
                           final_norm=(layer == depth - 1))
        x = x2d.reshape(batch, seq, d)
    if depth % 2 == 1:
        x = _final_norm(x, norm_final_g)
    return x


def _final_norm_kernel(x_ref, g_ref, out_ref):
    out_ref[...] = _rms_norm(x_ref[...], g_ref[...])


def _final_norm(x, g):
    batch, seq, d = x.shape
    x2d = x.reshape(batch * seq, d)
    tile = pl.BlockSpec((FFN_TM, d), lambda i: (i, 0))
    out = pl.pallas_call(
        _final_norm_kernel,
        grid=(x2d.shape[0] // FFN_TM,),
        in_specs=[tile, pl.BlockSpec((1, d), lambda i: (0, 0))],
        out_specs=tile,
        out_shape=jax.ShapeDtypeStruct(x2d.shape, F32),
        name="final_norm",
    )(x2d, g.reshape(1, -1))
    return out.reshape(batch, seq, d)
```

```python
import functools

import jax
import jax.numpy as jnp
from jax import lax
from jax.experimental import pallas as pl
from jax.experimental.pallas import tpu as pltpu

D_MODEL = 1024
W_BRANCH = 1024
SC_KERNEL = 3
CF_KERNEL = 31
POOL_WINDOWS = (2, 4, 8, 16)
POOL_GROUP = W_BRANCH // len(POOL_WINDOWS)
D_FF = 3584
N_EXPERTS = 8
TOP_K = 2
RMS_EPS = 1e-6
LN_EPS = 1e-5

COL_SC = 0
COL_CF = 3 * W_BRANCH
COL_POOL = 5 * W_BRANCH
COL_GATE = 6 * W_BRANCH
IN_COLS = 9 * W_BRANCH

SUBLANES = 8
LANES = 128
ROW_TILES = D_MODEL // LANES
assert ROW_TILES == SUBLANES
assert POOL_WINDOWS == tuple(2 ** (g + 1) for g in range(len(POOL_WINDOWS)))

HALO_SC = SUBLANES
HALO_CF = SUBLANES * (-(-CF_KERNEL // SUBLANES))
HALO_POOL = SUBLANES * len(POOL_WINDOWS)

MIX_TM = 256
CONV_ROWS = 64
WEIGHT_COLS = 512
FFN_TM = 512
FFN_FC = 512
MOE_TM = 512
ROUTE_TM = 512
COMBINE_TM = 512
VMEM_LIMIT = 60 * 1024 * 1024

F32 = jnp.float32
BF16 = jnp.bfloat16


def _rms_norm(x, g):
    ms = jnp.mean(x * x, axis=-1, keepdims=True)
    return x * lax.rsqrt(ms + RMS_EPS) * g


def _dot(a, b):
    return jnp.dot(a, b, preferred_element_type=F32)


def _col_blocks(w):
    k, n = w.shape
    return w.reshape(k, n // WEIGHT_COLS, WEIGHT_COLS).transpose(1, 0, 2).astype(BF16)


def _dot_blocks(a, w_ref):
    return jnp.concatenate([_dot(a, w_ref[c]) for c in range(w_ref.shape[0])], axis=-1)


def _mixer_kernel(x_ref, g_ref, w_sc_ref, w_cf_ref, w_pool_ref, w_gate_ref, scw_ref, cfw_ref,
                  cfb_ref, lng_ref, lnb_ref, poolw_ref, pscale_ref, wa_ref, wb_ref, wc_ref, wo_ref,
                  out_ref, u_buf, v_buf, p_buf, conv_buf):
    i = pl.program_id(1)
    tm = x_ref.shape[0]

    @pl.when(i == 0)
    def _():
        u_buf[0:HALO_SC, :] = jnp.zeros((HALO_SC, W_BRANCH), F32)
        v_buf[0:HALO_CF, :] = jnp.zeros((HALO_CF, W_BRANCH), F32)
        p_buf[0:HALO_POOL, :] = jnp.zeros((HALO_POOL, W_BRANCH), F32)

    x = x_ref[...]
    h = _rms_norm(x, g_ref[...]).astype(BF16)

    pb = _dot_blocks(h, w_cf_ref)
    v = pb[:, 0:W_BRANCH] * jax.nn.sigmoid(pb[:, W_BRANCH:2 * W_BRANCH])
    v_buf[HALO_CF:HALO_CF + tm, :] = v
    pa = _dot_blocks(h, w_sc_ref)
    pu = _dot_blocks(h, w_pool_ref)
    gate_logits = _dot_blocks(h, w_gate_ref)

    ext = CONV_ROWS + SUBLANES
    for row0 in range(0, tm, CONV_ROWS):
        for c in range(W_BRANCH // LANES):
            lanes = slice(c * LANES, (c + 1) * LANES)
            total = None
            for r in range(SUBLANES):
                part = None
                for q in range(-(-CF_KERNEL // SUBLANES)):
                    d = SUBLANES * q + r
                    if d < CF_KERNEL:
                        base = row0 + HALO_CF - SUBLANES * (q + 1)
                        term = (cfw_ref[CF_KERNEL - 1 - d:CF_KERNEL - d, lanes]
                                * v_buf[base:base + ext, lanes])
                        part = term if part is None else part + term
                part = part[SUBLANES - r:SUBLANES - r + CONV_ROWS, :]
                total = part if total is None else total + part
            conv_buf[row0:row0 + CONV_ROWS, lanes] = total + cfb_ref[:, lanes]
    acc = conv_buf[...]
    mu = jnp.mean(acc, axis=-1, keepdims=True)
    xc = acc - mu
    var = jnp.mean(xc * xc, axis=-1, keepdims=True)
    ln = xc * lax.rsqrt(var + LN_EPS) * lng_ref[...] + lnb_ref[...]
    y_b = _dot_blocks((ln * jax.nn.sigmoid(ln)).astype(BF16), wb_ref)
    v_buf[0:HALO_CF, :] = v_buf[tm:tm + HALO_CF, :]

    u = pa[:, W_BRANCH:2 * W_BRANCH] * pa[:, 2 * W_BRANCH:3 * W_BRANCH]
    u_buf[HALO_SC:HALO_SC + tm, :] = u
    conv = scw_ref[SC_KERNEL - 1:SC_KERNEL, :] * u
    for k in range(SC_KERNEL - 1):
        off = HALO_SC - (SC_KERNEL - 1) + k
        conv = conv + scw_ref[k:k + 1, :] * u_buf[off:off + tm, :]
    y_a = _dot_blocks((pa[:, 0:W_BRANCH] * conv).astype(BF16), wa_ref)
    u_buf[0:HALO_SC, :] = u_buf[tm:tm + HALO_SC, :]

    p_buf[HALO_POOL:HALO_POOL + tm, :] = pu
    pos = (i * tm + lax.broadcasted_iota(jnp.int32, (tm, 1), 0)).astype(F32)
    win_sums = {}
    level, w = p_buf[...], 1
    for gi in range(len(POOL_WINDOWS)):
        level = level[SUBLANES:, :] + level[SUBLANES - w:level.shape[0] - w, :]
        w *= 2
        first_row = HALO_POOL - SUBLANES * (gi + 1)
        win_sums[w] = level[first_row:first_row + tm, 0:POOL_GROUP]
        if gi + 1 < len(POOL_WINDOWS):
            level = level[:, POOL_GROUP:]
    q_groups = []
    for gi, w in enumerate(POOL_WINDOWS):
        tok = pu[:, gi * POOL_GROUP:(gi + 1) * POOL_GROUP]
        inv_count = 1.0 / jnp.minimum(pos + 1.0, float(w))
        q_groups.append(_dot((win_sums[w] * inv_count - tok).astype(BF16), poolw_ref[gi]))
    q = jnp.concatenate(q_groups, axis=-1) * pscale_ref[...]
    y_c = _dot_blocks(q.astype(BF16), wc_ref)
    p_buf[0:HALO_POOL, :] = p_buf[tm:tm + HALO_POOL, :]

    gates = jax.nn.sigmoid(gate_logits)
    merged = (gates[:, 0:D_MODEL] * y_a + gates[:, D_MODEL:2 * D_MODEL] * y_b
              + gates[:, 2 * D_MODEL:3 * D_MODEL] * y_c)
    out_ref[...] = x + _dot_blocks(merged.astype(BF16), wo_ref)


def _resident(shape):
    zeros = (0,) * len(shape)
    return pl.BlockSpec(shape, lambda b, i: zeros, pipeline_mode=pl.Buffered(1))


def _mixer(x, g, w_in, scw, cfw, cfb, lng, lnb, poolw, pscale, wa, wb, wc, wo):
    batch, seq, _ = x.shape
    tm = MIX_TM
    row = lambda v: v.reshape(1, -1)
    tile = pl.BlockSpec((None, tm, D_MODEL), lambda b, i: (b, i, 0))
    w_sc, w_cf, w_pool, w_gate = (
        _col_blocks(w_in[:, lo:hi]) for lo, hi in
        ((COL_SC, COL_CF), (COL_CF, COL_POOL), (COL_POOL, COL_GATE), (COL_GATE, IN_COLS)))
    operands = (row(g), w_sc, w_cf, w_pool, w_gate, scw, cfw, row(cfb), row(lng), row(lnb),
                poolw.astype(BF16), row(pscale), _col_blocks(wa), _col_blocks(wb),
                _col_blocks(wc), _col_blocks(wo))
    return pl.pallas_call(
        _mixer_kernel,
        grid=(batch, seq // tm),
        in_specs=[tile] + [_resident(op.shape) for op in operands],
        out_specs=tile,
        out_shape=jax.ShapeDtypeStruct(x.shape, F32),
        scratch_shapes=[pltpu.VMEM((HALO_SC + tm, W_BRANCH), F32),
                        pltpu.VMEM((HALO_CF + tm, W_BRANCH), F32),
                        pltpu.VMEM((HALO_POOL + tm, W_BRANCH), F32),
                        pltpu.VMEM((tm, W_BRANCH), F32)],
        compiler_params=pltpu.CompilerParams(
            dimension_semantics=("arbitrary", "arbitrary"),
            vmem_limit_bytes=VMEM_LIMIT),
        name="mixer",
    )(x, *operands)


class _ExpertWeights:
    def __init__(self, w1_hbm, w3_hbm, w2_hbm, w1r, w3r, w2r, stage13, stage2, sem):
        self.hbm = (w1_hbm, w3_hbm, w2_hbm)
        self.w1r, self.w3r, self.w2r = w1r, w3r, w2r
        self.stage13, self.stage2, self.sem = stage13, stage2, sem
        self.n_steps, _, self.fc = w1r.shape

    def _copies(self, e, k):
        w1_hbm, w3_hbm, w2_hbm = self.hbm
        cols = slice(k * self.fc, (k + 1) * self.fc)
        return (pltpu.make_async_copy(w1_hbm.at[e, :, cols], self.stage13.at[0], self.sem.at[0]),
                pltpu.make_async_copy(w3_hbm.at[e, :, cols], self.stage13.at[1], self.sem.at[1]),
                pltpu.make_async_copy(w2_hbm.at[e, cols, :], self.stage2, self.sem.at[2]))

    def start_fetch(self, e, f):
        if isinstance(f, int):
            for copy in self._copies(e, f):
                copy.start()
            return
        for k in range(self.n_steps):
            @pl.when(f == k)
            def _():
                for copy in self._copies(e, k):
                    copy.start()

    def finish_fetch(self, f):
        for copy in self._copies(0, 0):
            copy.wait()
        self.w1r[f] = self.stage13[0].astype(BF16)
        self.w3r[f] = self.stage13[1].astype(BF16)
        self.w2r[f] = self.stage2[...].astype(BF16)

    def swiglu(self, h, f):
        a = _dot(h, self.w1r[f])
        c = _dot(h, self.w3r[f])
        return _dot((a * jax.nn.sigmoid(a) * c).astype(BF16), self.w2r[f])


def _expert_weight_scratch(n_steps, fc):
    return [pltpu.VMEM((n_steps, D_MODEL, fc), BF16), pltpu.VMEM((n_steps, D_MODEL, fc), BF16),
            pltpu.VMEM((n_steps, fc, D_MODEL), BF16), pltpu.VMEM((2, D_MODEL, fc), F32),
            pltpu.VMEM((fc, D_MODEL), F32), pltpu.SemaphoreType.DMA((3,))]


def _dense_ffn_kernel(x_ref, g_ref, w1_hbm, w3_hbm, w2_hbm, out_ref, h_buf, *wbufs):
    weights = _ExpertWeights(w1_hbm, w3_hbm, w2_hbm, *wbufs)

    @pl.when(pl.program_id(0) == 0)
    def _():
        for k in range(weights.n_steps):
            weights.start_fetch(0, k)
            weights.finish_fetch(k)

    x = x_ref[...]
    h_buf[...] = _rms_norm(x, g_ref[...]).astype(BF16)
    acc = x
    for k in range(weights.n_steps):
        acc = acc + weights.swiglu(h_buf[...], k)
    out_ref[...] = acc


def _dense_ffn(x2d, g, w1, w3, w2):
    tokens = x2d.shape[0]
    tm, fc = FFN_TM, FFN_FC
    n_steps = D_FF // fc
    tile = pl.BlockSpec((tm, D_MODEL), lambda i: (i, 0))
    hbm = pl.BlockSpec(memory_space=pl.ANY)
    return pl.pallas_call(
        _dense_ffn_kernel,
        grid=(tokens // tm,),
        in_specs=[tile, pl.BlockSpec((1, D_MODEL), lambda i: (0, 0)), hbm, hbm, hbm],
        out_specs=tile,
        out_shape=jax.ShapeDtypeStruct(x2d.shape, F32),
        scratch_shapes=[pltpu.VMEM((tm, D_MODEL), BF16)] + _expert_weight_scratch(n_steps, fc),
        compiler_params=pltpu.CompilerParams(
            dimension_semantics=("arbitrary",),
            vmem_limit_bytes=VMEM_LIMIT),
        name="dense_ffn",
    )(x2d, g.reshape(1, -1), w1[None], w3[None], w2[None])


def _store_token_tiles(ref, value):
    rows = value.shape[0]
    for c in range(ROW_TILES):
        ref[pl.ds(c, rows, stride=ROW_TILES), :] = value[:, c * LANES:(c + 1) * LANES]


def _load_token_tiles(ref, rows):
    return [ref[pl.ds(c, rows, stride=ROW_TILES), :] for c in range(ROW_TILES)]


def _router_kernel(x_ref, g_ref, router_ref, ids_ref, probs_ref, ht_ref):
    h = _rms_norm(x_ref[...], g_ref[...])
    _store_token_tiles(ht_ref, h)
    logits = jnp.dot(h, router_ref[...], preferred_element_type=F32,
                     precision=lax.Precision.HIGHEST)
    lane = lax.broadcasted_iota(jnp.int32, logits.shape, 1)
    m1 = jnp.max(logits, axis=-1, keepdims=True)
    i1 = jnp.min(jnp.where(logits == m1, lane, N_EXPERTS), axis=-1, keepdims=True)
    rest = jnp.where(lane == i1, -jnp.inf, logits)
    m2 = jnp.max(rest, axis=-1, keepdims=True)
    i2 = jnp.min(jnp.where(rest == m2, lane, N_EXPERTS), axis=-1, keepdims=True)
    e2 = jnp.exp(m2 - m1)
    denom = 1.0 + e2
    first = lax.broadcasted_iota(jnp.int32, ids_ref.shape, 1) == 0
    ids_ref[...] = jnp.where(first, i1, i2)
    probs_ref[...] = jnp.where(first, 1.0 / denom, e2 / denom)


def _router(x2d, g, router):
    tokens = x2d.shape[0]
    tm = ROUTE_TM
    pair = pl.BlockSpec((tm, TOP_K), lambda i: (i, 0))
    return pl.pallas_call(
        _router_kernel,
        grid=(tokens // tm,),
        in_specs=[pl.BlockSpec((tm, D_MODEL), lambda i: (i, 0)),
                  pl.BlockSpec((1, D_MODEL), lambda i: (0, 0)),
                  pl.BlockSpec((D_MODEL, N_EXPERTS), lambda i: (0, 0))],
        out_specs=[pair, pair, pl.BlockSpec((tm * ROW_TILES, LANES), lambda i: (i, 0))],
        out_shape=[jax.ShapeDtypeStruct((tokens, TOP_K), jnp.int32),
                   jax.ShapeDtypeStruct((tokens, TOP_K), F32),
                   jax.ShapeDtypeStruct((tokens * ROW_TILES, LANES), F32)],
        compiler_params=pltpu.CompilerParams(dimension_semantics=("arbitrary",)),
        name="router",
    )(x2d, g.reshape(1, -1), router)


def _routing_tables(ids, tm):
    tokens = ids.shape[0]
    n = TOP_K * tokens
    n_pad = N_EXPERTS * tm
    n_tiles = (n + n_pad) // tm
    e_flat = ids.T.reshape(n)
    experts = jnp.arange(N_EXPERTS, dtype=jnp.int32)
    counts = jnp.sum((e_flat[:, None] == experts[None, :]).astype(jnp.int32), axis=0)
    tiles_per = (counts + tm - 1) // tm
    tile_end = jnp.cumsum(tiles_per)
    n_used = tile_end[-1]
    jt = jnp.arange(n_tiles, dtype=jnp.int32)
    tile_expert = jnp.minimum(
        jnp.sum((jt[:, None] >= tile_end[None, :]).astype(jnp.int32), axis=1), N_EXPERTS - 1)
    pad_end = jnp.cumsum(tiles_per * tm - counts)
    pad_key = jnp.sum((jnp.arange(n_pad, dtype=jnp.int32)[:, None] >= pad_end[None, :])
                      .astype(jnp.int32), axis=1)
    row_dst = jnp.argsort(jnp.concatenate([e_flat, pad_key]), stable=True).astype(jnp.int32)
    row_token = jnp.where(row_dst < n, row_dst % tokens, 0).astype(jnp.int32)
    return tile_expert.astype(jnp.int32), n_used.reshape(1).astype(jnp.int32), row_token, row_dst


def _moe_group_kernel(te_ref, nu_ref, tok_ref, dst_ref,
                      ht_hbm, w1_hbm, w3_hbm, w2_hbm, y_hbm,
                      xbuf, h_buf, acc_buf, obuf, gather_sem, scatter_sem, *wbufs):
    weights = _ExpertWeights(w1_hbm, w3_hbm, w2_hbm, *wbufs)
    n_steps = weights.n_steps
    j = pl.program_id(0)
    f = pl.program_id(1)
    n_used = nu_ref[0]
    tm = h_buf.shape[0]
    chunk = tm // n_steps
    tail_rows = range(n_steps * chunk, tm)
    used = j < n_used
    has_prev = j > 0
    has_next = j + 1 < n_used
    slot = lax.rem(j, 2)

    def row_tile(r):
        return pl.ds(pl.multiple_of(r * ROW_TILES, ROW_TILES), ROW_TILES)

    def gather_row(tile, r, queue=0):
        pltpu.make_async_copy(ht_hbm.at[row_tile(tok_ref[tile * tm + r]), :],
                              xbuf.at[row_tile(r), :], gather_sem).start(priority=queue)

    def scatter_row(tile, r, queue=0):
        s = lax.rem(tile, 2)
        pltpu.make_async_copy(obuf.at[s].at[row_tile(r), :],
                              y_hbm.at[row_tile(dst_ref[tile * tm + r]), :],
                              scatter_sem.at[s]).start(priority=queue)

    def wait_gather():
        pltpu.make_async_copy(ht_hbm.at[pl.ds(0, tm * ROW_TILES), :], xbuf, gather_sem).wait()

    def wait_scatter(s):
        pltpu.make_async_copy(obuf.at[s], y_hbm.at[pl.ds(0, tm * ROW_TILES), :],
                              scatter_sem.at[s]).wait()

    def for_all_rows(row_fn, tile):
        def body(r, carry):
            row_fn(tile, r)
            return carry
        lax.fori_loop(0, tm, body, 0)

    @pl.when(jnp.logical_and(f == 0, j == 0))
    def _():
        for_all_rows(gather_row, 0)

    @pl.when(jnp.logical_and(f == 0, used))
    def _():
        wait_gather()
        for c, part in enumerate(_load_token_tiles(xbuf, tm)):
            h_buf[:, c * LANES:(c + 1) * LANES] = part.astype(BF16)
        acc_buf[...] = jnp.zeros_like(acc_buf)

    @pl.when(jnp.logical_and(f == 0, has_next))
    def _():
        for r in tail_rows:
            gather_row(j + 1, r)

    @pl.when(jnp.logical_and(f == 0, jnp.logical_and(used, has_prev)))
    def _():
        for r in tail_rows:
            scatter_row(j - 1, r)

    next_expert = te_ref[jnp.minimum(j + 1, pl.num_programs(0) - 1)]
    expert_ends = jnp.logical_and(has_next, next_expert != te_ref[j])

    @pl.when(j == 0)
    def _():
        weights.start_fetch(te_ref[0], f)
        weights.finish_fetch(f)

    @pl.when(expert_ends)
    def _():
        weights.start_fetch(next_expert, f)

    def step(do_gather, do_scatter):
        base = f * chunk
        for k in range(chunk):
            if do_gather:
                gather_row(j + 1, base + k, queue=k % 2)
            if do_scatter:
                scatter_row(j - 1, base + k, queue=k % 2)
        acc_buf[...] += weights.swiglu(h_buf[...], f)

    @pl.when(jnp.logical_and(has_prev, has_next))
    def _():
        step(True, True)

    @pl.when(jnp.logical_and(jnp.logical_not(has_prev), has_next))
    def _():
        step(True, False)

    @pl.when(jnp.logical_and(used, jnp.logical_and(has_prev, jnp.logical_not(has_next))))
    def _():
        step(False, True)

    @pl.when(jnp.logical_and(used, jnp.logical_not(jnp.logical_or(has_prev, has_next))))
    def _():
        step(False, False)

    @pl.when(expert_ends)
    def _():
        weights.finish_fetch(f)

    @pl.when(jnp.logical_and(f == n_steps - 1, used))
    def _():
        @pl.when(j >= 2)
        def _():
            wait_scatter(slot)
        _store_token_tiles(obuf.at[slot], acc_buf[...])

        @pl.when(jnp.logical_not(has_next))
        def _():
            for_all_rows(scatter_row, j)
            wait_scatter(slot)

            @pl.when(has_prev)
            def _():
                wait_scatter(1 - slot)

    @pl.when(jnp.logical_and(f == 0, jnp.logical_not(used)))
    def _():
        obuf[0] = jnp.zeros(obuf.shape[1:], F32)
        rows = tm * ROW_TILES
        fill = pltpu.make_async_copy(obuf.at[0],
                                     y_hbm.at[pl.ds(pl.multiple_of(j * rows, rows), rows), :],
                                     scatter_sem.at[0])
        fill.start()
        fill.wait()


def _moe_grouped(ht, w1, w3, w2, tables):
    tile_expert, n_used, row_token, row_dst = tables
    tm, fc = MOE_TM, FFN_FC
    n_rows = row_token.shape[0]
    tile_rows = tm * ROW_TILES
    n_steps = D_FF // fc
    hbm = pl.BlockSpec(memory_space=pl.ANY)
    grid_spec = pltpu.PrefetchScalarGridSpec(
        num_scalar_prefetch=4,
        grid=(n_rows // tm, n_steps),
        in_specs=[hbm, hbm, hbm, hbm],
        out_specs=hbm,
        scratch_shapes=[pltpu.VMEM((tile_rows, LANES), F32), pltpu.VMEM((tm, D_MODEL), BF16),
                        pltpu.VMEM((tm, D_MODEL), F32), pltpu.VMEM((2, tile_rows, LANES), F32),
                        pltpu.SemaphoreType.DMA(()), pltpu.SemaphoreType.DMA((2,))]
        + _expert_weight_scratch(n_steps, fc),
    )
    return pl.pallas_call(
        _moe_group_kernel,
        grid_spec=grid_spec,
        out_shape=jax.ShapeDtypeStruct((n_rows * ROW_TILES, LANES), F32),
        compiler_params=pltpu.CompilerParams(
            dimension_semantics=("arbitrary", "arbitrary"),
            vmem_limit_bytes=VMEM_LIMIT),
        name="moe_grouped",
    )(tile_expert, n_used, row_token, row_dst, ht, w1, w3, w2)


def _combine_kernel(x_ref, y0_ref, y1_ref, p_ref, gf_ref, out_ref, *, final_norm):
    tm = x_ref.shape[0]
    p = p_ref[...]
    y0 = _load_token_tiles(y0_ref, tm)
    y1 = _load_token_tiles(y1_ref, tm)
    y = jnp.concatenate(
        [x_ref[:, c * LANES:(c + 1) * LANES] + p[:, 0:1] * y0[c] + p[:, 1:2] * y1[c]
         for c in range(ROW_TILES)], axis=-1)
    out_ref[...] = _rms_norm(y, gf_ref[...]) if final_norm else y


def _combine(x2d, y_tiles, probs, g_final, *, final_norm):
    tokens = x2d.shape[0]
    tm = COMBINE_TM
    tile = pl.BlockSpec((tm, D_MODEL), lambda i: (i, 0))
    first = pl.BlockSpec((tm * ROW_TILES, LANES), lambda i: (i, 0))
    second = pl.BlockSpec((tm * ROW_TILES, LANES), lambda i: (i + tokens // tm, 0))
    return pl.pallas_call(
        functools.partial(_combine_kernel, final_norm=final_norm),
        grid=(tokens // tm,),
        in_specs=[tile, first, second,
                  pl.BlockSpec((tm, TOP_K), lambda i: (i, 0)),
                  pl.BlockSpec((1, D_MODEL), lambda i: (0, 0))],
        out_specs=tile,
        out_shape=jax.ShapeDtypeStruct(x2d.shape, F32),
        compiler_params=pltpu.CompilerParams(dimension_semantics=("arbitrary",)),
        name="moe_combine",
    )(x2d, y_tiles, y_tiles, probs, g_final.reshape(1, -1))


def _moe_ffn(x2d, g, g_final, router, w1, w3, w2, *, final_norm):
    ids, probs, ht = _router(x2d, g, router)
    tables = _routing_tables(ids, MOE_TM)
    y_tiles = _moe_grouped(ht, w1, w3, w2, tables)
    return _combine(x2d, y_tiles, probs, g_final, final_norm=final_norm)


def kernel(x, norm_mix_g, w_in, sc_conv_w, cf_conv_w, cf_conv_b, cf_ln_g, cf_ln_b,
           pool_w, pool_scale, w_sc_out, w_cf_out, w_pool_out, w_o, norm_ffn_g,
           dense_w1, dense_w3, dense_w2, moe_router, moe_w1, moe_w3, moe_w2,
           norm_final_g):
    batch, seq, d = x.shape
    depth = w_in.shape[0]
    for layer in range(depth):
        x = _mixer(x, norm_mix_g[layer], w_in[layer], sc_conv_w[layer], cf_conv_w[layer],
                   cf_conv_b[layer], cf_ln_g[layer], cf_ln_b[layer], pool_w[layer],
                   pool_scale[layer], w_sc_out[layer], w_cf_out[layer], w_pool_out[layer],
                   w_o[layer])
        x2d = x.reshape(batch * seq, d)
        i = layer // 2
        if layer % 2 == 0:
            x2d = _dense_ffn(x2d, norm_ffn_g[layer], dense_w1[i], dense_w3[i], dense_w2[i])
        else:
            x2d = _moe_ffn(x2d, norm_ffn_g[layer], norm_final_g, moe_router[i],
                           moe_w1[i], moe_w3[i], moe_w2[i],
                           final_norm=(layer == depth - 1))
        x = x2d.reshape(batch, seq, d)
    if depth % 2 == 1:
        x = _final_norm(x, norm_final_g)
    return x


def _final_norm_kernel(x_ref, g_ref, out_ref):
    out_ref[...] = _rms_norm(x_ref[...], g_ref[...])


def _final_norm(x, g):
    batch, seq, d = x.shape
    x2d = x.reshape(batch * seq, d)
    tile = pl.BlockSpec((FFN_TM, d), lambda i: (i, 0))
    out = pl.pallas_call(
        _final_norm_kernel,
        grid=(x2d.shape[0] // FFN_TM,),
        in_specs=[tile, pl.BlockSpec((1, d), lambda i: (0, 0))],
        out_specs=tile,
        out_shape=jax.ShapeDtypeStruct(x2d.shape, F32),
        name="final_norm",
    )(x2d, g.reshape(1, -1))
    return out.reshape(batch, seq, d)
```

```python
import functools

import jax
import jax.numpy as jnp
from jax import lax
from jax.experimental import pallas as pl
from jax.experimental.pallas import tpu as pltpu

D_MODEL = 1024
W_BRANCH = 1024
SC_KERNEL = 3
CF_KERNEL = 31
POOL_WINDOWS = (2, 4, 8, 16)
POOL_GROUP = W_BRANCH // len(POOL_WINDOWS)
D_FF = 3584
N_EXPERTS = 8
TOP_K = 2
RMS_EPS = 1e-6
LN_EPS = 1e-5

COL_SC = 0
COL_CF = 3 * W_BRANCH
COL_POOL = 5 * W_BRANCH
COL_GATE = 6 * W_BRANCH
IN_COLS = 9 * W_BRANCH

SUBLANES = 8
LANES = 128
ROW_TILES = D_MODEL // LANES
assert ROW_TILES == SUBLANES
assert POOL_WINDOWS == tuple(2 ** (g + 1) for g in range(len(POOL_WINDOWS)))

HALO_SC = SUBLANES
HALO_CF = SUBLANES * (-(-CF_KERNEL // SUBLANES))
HALO_POOL = SUBLANES * len(POOL_WINDOWS)

MIX_TM = 256
CONV_ROWS = 64
WEIGHT_COLS = 512
N_MIX_WEIGHTS = 8
FFN_TM = 512
FFN_FC = 512
MOE_TM = 512
ROUTE_TM = 512
COMBINE_TM = 512
VMEM_LIMIT = 60 * 1024 * 1024

F32 = jnp.float32
BF16 = jnp.bfloat16


def _rms_norm(x, g):
    ms = jnp.mean(x * x, axis=-1, keepdims=True)
    return x * lax.rsqrt(ms + RMS_EPS) * g


def _dot(a, b):
    return jnp.dot(a, b, preferred_element_type=F32)


def _dot_blocks(a, w_ref):
    return jnp.concatenate([_dot(a, w_ref[c]) for c in range(w_ref.shape[0])], axis=-1)


def _mixer_kernel(x_ref, g_ref, scw_ref, cfw_ref, cfb_ref, lng_ref, lnb_ref, poolw_ref,
                  pscale_ref, *rest):
    weights_hbm = rest[:N_MIX_WEIGHTS]
    out_ref, u_buf, v_buf, p_buf, conv_buf = rest[N_MIX_WEIGHTS:N_MIX_WEIGHTS + 5]
    weights_vmem = rest[N_MIX_WEIGHTS + 5:2 * N_MIX_WEIGHTS + 5]
    weight_sem = rest[2 * N_MIX_WEIGHTS + 5]
    w_sc_ref, w_cf_ref, w_pool_ref, w_gate_ref, wa_ref, wb_ref, wc_ref, wo_ref = weights_vmem
    i = pl.program_id(1)
    tm = x_ref.shape[0]

    @pl.when(jnp.logical_and(pl.program_id(0) == 0, i == 0))
    def _():
        copies = []
        for n, (hbm, vmem) in enumerate(zip(weights_hbm, weights_vmem)):
            for c in range(vmem.shape[0]):
                copies.append(pltpu.make_async_copy(
                    hbm.at[:, c * WEIGHT_COLS:(c + 1) * WEIGHT_COLS], vmem.at[c], weight_sem.at[n]))
        for copy in copies:
            copy.start()
        for copy in copies:
            copy.wait()

    @pl.when(i == 0)
    def _():
        u_buf[0:HALO_SC, :] = jnp.zeros((HALO_SC, W_BRANCH), F32)
        v_buf[0:HALO_CF, :] = jnp.zeros((HALO_CF, W_BRANCH), F32)
        p_buf[0:HALO_POOL, :] = jnp.zeros((HALO_POOL, W_BRANCH), F32)

    x = x_ref[...]
    h = _rms_norm(x, g_ref[...]).astype(BF16)

    pb = _dot_blocks(h, w_cf_ref)
    v = pb[:, 0:W_BRANCH] * jax.nn.sigmoid(pb[:, W_BRANCH:2 * W_BRANCH])
    v_buf[HALO_CF:HALO_CF + tm, :] = v
    pa = _dot_blocks(h, w_sc_ref)
    pu = _dot_blocks(h, w_pool_ref)
    gate_logits = _dot_blocks(h, w_gate_ref)

    ext = CONV_ROWS + SUBLANES
    for row0 in range(0, tm, CONV_ROWS):
        for c in range(W_BRANCH // LANES):
            lanes = slice(c * LANES, (c + 1) * LANES)
            total = None
            for r in range(SUBLANES):
                part = None
                for q in range(-(-CF_KERNEL // SUBLANES)):
                    d = SUBLANES * q + r
                    if d < CF_KERNEL:
                        base = row0 + HALO_CF - SUBLANES * (q + 1)
                        term = (cfw_ref[CF_KERNEL - 1 - d:CF_KERNEL - d, lanes]
                                * v_buf[base:base + ext, lanes])
                        part = term if part is None else part + term
                part = part[SUBLANES - r:SUBLANES - r + CONV_ROWS, :]
                total = part if total is None else total + part
            conv_buf[row0:row0 + CONV_ROWS, lanes] = total + cfb_ref[:, lanes]
    acc = conv_buf[...]
    mu = jnp.mean(acc, axis=-1, keepdims=True)
    xc = acc - mu
    var = jnp.mean(xc * xc, axis=-1, keepdims=True)
    ln = xc * lax.rsqrt(var + LN_EPS) * lng_ref[...] + lnb_ref[...]
    y_b = _dot_blocks((ln * jax.nn.sigmoid(ln)).astype(BF16), wb_ref)
    v_buf[0:HALO_CF, :] = v_buf[tm:tm + HALO_CF, :]

    u = pa[:, W_BRANCH:2 * W_BRANCH] * pa[:, 2 * W_BRANCH:3 * W_BRANCH]
    u_buf[HALO_SC:HALO_SC + tm, :] = u
    conv = scw_ref[SC_KERNEL - 1:SC_KERNEL, :] * u
    for k in range(SC_KERNEL - 1):
        off = HALO_SC - (SC_KERNEL - 1) + k
        conv = conv + scw_ref[k:k + 1, :] * u_buf[off:off + tm, :]
    y_a = _dot_blocks((pa[:, 0:W_BRANCH] * conv).astype(BF16), wa_ref)
    u_buf[0:HALO_SC, :] = u_buf[tm:tm + HALO_SC, :]

    p_buf[HALO_POOL:HALO_POOL + tm, :] = pu
    pos = (i * tm + lax.broadcasted_iota(jnp.int32, (tm, 1), 0)).astype(F32)
    win_sums = {}
    level, w = p_buf[...], 1
    for gi in range(len(POOL_WINDOWS)):
        level = level[SUBLANES:, :] + level[SUBLANES - w:level.shape[0] - w, :]
        w *= 2
        first_row = HALO_POOL - SUBLANES * (gi + 1)
        win_sums[w] = level[first_row:first_row + tm, 0:POOL_GROUP]
        if gi + 1 < len(POOL_WINDOWS):
            level = level[:, POOL_GROUP:]
    q_groups = []
    for gi, w in enumerate(POOL_WINDOWS):
        tok = pu[:, gi * POOL_GROUP:(gi + 1) * POOL_GROUP]
        inv_count = 1.0 / jnp.minimum(pos + 1.0, float(w))
        q_groups.append(_dot((win_sums[w] * inv_count - tok).astype(BF16), poolw_ref[gi]))
    q = jnp.concatenate(q_groups, axis=-1) * pscale_ref[...]
    y_c = _dot_blocks(q.astype(BF16), wc_ref)
    p_buf[0:HALO_POOL, :] = p_buf[tm:tm + HALO_POOL, :]

    gates = jax.nn.sigmoid(gate_logits)
    merged = (gates[:, 0:D_MODEL] * y_a + gates[:, D_MODEL:2 * D_MODEL] * y_b
              + gates[:, 2 * D_MODEL:3 * D_MODEL] * y_c)
    out_ref[...] = x + _dot_blocks(merged.astype(BF16), wo_ref)


def _resident(shape):
    zeros = (0,) * len(shape)
    return pl.BlockSpec(shape, lambda b, i: zeros, pipeline_mode=pl.Buffered(1))


def _mixer(x, g, w_in, scw, cfw, cfb, lng, lnb, poolw, pscale, wa, wb, wc, wo):
    batch, seq, _ = x.shape
    tm = MIX_TM
    row = lambda v: v.reshape(1, -1)
    tile = pl.BlockSpec((None, tm, D_MODEL), lambda b, i: (b, i, 0))
    w_groups = [w_in[:, lo:hi] for lo, hi in
                ((COL_SC, COL_CF), (COL_CF, COL_POOL), (COL_POOL, COL_GATE), (COL_GATE, IN_COLS))]
    weights = [w.astype(BF16) for w in w_groups + [wa, wb, wc, wo]]
    assert len(weights) == N_MIX_WEIGHTS
    small = (row(g), scw, cfw, row(cfb), row(lng), row(lnb), poolw.astype(BF16), row(pscale))
    return pl.pallas_call(
        _mixer_kernel,
        grid=(batch, seq // tm),
        in_specs=[tile] + [_resident(op.shape) for op in small]
        + [pl.BlockSpec(memory_space=pl.ANY)] * N_MIX_WEIGHTS,
        out_specs=tile,
        out_shape=jax.ShapeDtypeStruct(x.shape, F32),
        scratch_shapes=[pltpu.VMEM((HALO_SC + tm, W_BRANCH), F32),
                        pltpu.VMEM((HALO_CF + tm, W_BRANCH), F32),
                        pltpu.VMEM((HALO_POOL + tm, W_BRANCH), F32),
                        pltpu.VMEM((tm, W_BRANCH), F32)]
        + [pltpu.VMEM((w.shape[1] // WEIGHT_COLS, w.shape[0], WEIGHT_COLS), BF16) for w in weights]
        + [pltpu.SemaphoreType.DMA((N_MIX_WEIGHTS,))],
        compiler_params=pltpu.CompilerParams(
            dimension_semantics=("arbitrary", "arbitrary"),
            vmem_limit_bytes=VMEM_LIMIT),
        name="mixer",
    )(x, *small, *weights)


class _ExpertWeights:
    def __init__(self, w1_hbm, w3_hbm, w2_hbm, w1r, w3r, w2r, stage13, stage2, sem):
        self.hbm = (w1_hbm, w3_hbm, w2_hbm)
        self.w1r, self.w3r, self.w2r = w1r, w3r, w2r
        self.stage13, self.stage2, self.sem = stage13, stage2, sem
        self.n_steps, _, self.fc = w1r.shape

    def _copies(self, e, k):
        w1_hbm, w3_hbm, w2_hbm = self.hbm
        cols = slice(k * self.fc, (k + 1) * self.fc)
        return (pltpu.make_async_copy(w1_hbm.at[e, :, cols], self.stage13.at[0], self.sem.at[0]),
                pltpu.make_async_copy(w3_hbm.at[e, :, cols], self.stage13.at[1], self.sem.at[1]),
                pltpu.make_async_copy(w2_hbm.at[e, cols, :], self.stage2, self.sem.at[2]))

    def start_fetch(self, e, f):
        if isinstance(f, int):
            for copy in self._copies(e, f):
                copy.start()
            return
        for k in range(self.n_steps):
            @pl.when(f == k)
            def _():
                for copy in self._copies(e, k):
                    copy.start()

    def finish_fetch(self, f):
        for copy in self._copies(0, 0):
            copy.wait()
        self.w1r[f] = self.stage13[0].astype(BF16)
        self.w3r[f] = self.stage13[1].astype(BF16)
        self.w2r[f] = self.stage2[...].astype(BF16)

    def swiglu(self, h, f):
        a = _dot(h, self.w1r[f])
        c = _dot(h, self.w3r[f])
        return _dot((a * jax.nn.sigmoid(a) * c).astype(BF16), self.w2r[f])


def _expert_weight_scratch(n_steps, fc):
    return [pltpu.VMEM((n_steps, D_MODEL, fc), BF16), pltpu.VMEM((n_steps, D_MODEL, fc), BF16),
            pltpu.VMEM((n_steps, fc, D_MODEL), BF16), pltpu.VMEM((2, D_MODEL, fc), F32),
            pltpu.VMEM((fc, D_MODEL), F32), pltpu.SemaphoreType.DMA((3,))]


def _dense_ffn_kernel(x_ref, g_ref, w1_hbm, w3_hbm, w2_hbm, out_ref, h_buf, *wbufs):
    weights = _ExpertWeights(w1_hbm, w3_hbm, w2_hbm, *wbufs)

    @pl.when(pl.program_id(0) == 0)
    def _():
        for k in range(weights.n_steps):
            weights.start_fetch(0, k)
            weights.finish_fetch(k)

    x = x_ref[...]
    h_buf[...] = _rms_norm(x, g_ref[...]).astype(BF16)
    acc = x
    for k in range(weights.n_steps):
        acc = acc + weights.swiglu(h_buf[...], k)
    out_ref[...] = acc


def _dense_ffn(x2d, g, w1, w3, w2):
    tokens = x2d.shape[0]
    tm, fc = FFN_TM, FFN_FC
    n_steps = D_FF // fc
    tile = pl.BlockSpec((tm, D_MODEL), lambda i: (i, 0))
    hbm = pl.BlockSpec(memory_space=pl.ANY)
    return pl.pallas_call(
        _dense_ffn_kernel,
        grid=(tokens // tm,),
        in_specs=[tile, pl.BlockSpec((1, D_MODEL), lambda i: (0, 0)), hbm, hbm, hbm],
        out_specs=tile,
        out_shape=jax.ShapeDtypeStruct(x2d.shape, F32),
        scratch_shapes=[pltpu.VMEM((tm, D_MODEL), BF16)] + _expert_weight_scratch(n_steps, fc),
        compiler_params=pltpu.CompilerParams(
            dimension_semantics=("arbitrary",),
            vmem_limit_bytes=VMEM_LIMIT),
        name="dense_ffn",
    )(x2d, g.reshape(1, -1), w1[None], w3[None], w2[None])


def _store_token_tiles(ref, value):
    rows = value.shape[0]
    for c in range(ROW_TILES):
        ref[pl.ds(c, rows, stride=ROW_TILES), :] = value[:, c * LANES:(c + 1) * LANES]


def _load_token_tiles(ref, rows):
    return [ref[pl.ds(c, rows, stride=ROW_TILES), :] for c in range(ROW_TILES)]


def _router_kernel(x_ref, g_ref, router_ref, ids_ref, probs_ref, ht_ref):
    h = _rms_norm(x_ref[...], g_ref[...])
    _store_token_tiles(ht_ref, h)
    logits = jnp.dot(h, router_ref[...], preferred_element_type=F32,
                     precision=lax.Precision.HIGHEST)
    lane = lax.broadcasted_iota(jnp.int32, logits.shape, 1)
    m1 = jnp.max(logits, axis=-1, keepdims=True)
    i1 = jnp.min(jnp.where(logits == m1, lane, N_EXPERTS), axis=-1, keepdims=True)
    rest = jnp.where(lane == i1, -jnp.inf, logits)
    m2 = jnp.max(rest, axis=-1, keepdims=True)
    i2 = jnp.min(jnp.where(rest == m2, lane, N_EXPERTS), axis=-1, keepdims=True)
    e2 = jnp.exp(m2 - m1)
    denom = 1.0 + e2
    first = lax.broadcasted_iota(jnp.int32, ids_ref.shape, 1) == 0
    ids_ref[...] = jnp.where(first, i1, i2)
    probs_ref[...] = jnp.where(first, 1.0 / denom, e2 / denom)


def _router(x2d, g, router):
    tokens = x2d.shape[0]
    tm = ROUTE_TM
    pair = pl.BlockSpec((tm, TOP_K), lambda i: (i, 0))
    return pl.pallas_call(
        _router_kernel,
        grid=(tokens // tm,),
        in_specs=[pl.BlockSpec((tm, D_MODEL), lambda i: (i, 0)),
                  pl.BlockSpec((1, D_MODEL), lambda i: (0, 0)),
                  pl.BlockSpec((D_MODEL, N_EXPERTS), lambda i: (0, 0))],
        out_specs=[pair, pair, pl.BlockSpec((tm * ROW_TILES, LANES), lambda i: (i, 0))],
        out_shape=[jax.ShapeDtypeStruct((tokens, TOP_K), jnp.int32),
                   jax.ShapeDtypeStruct((tokens, TOP_K), F32),
                   jax.ShapeDtypeStruct((tokens * ROW_TILES, LANES), F32)],
        compiler_params=pltpu.CompilerParams(dimension_semantics=("arbitrary",)),
        name="router",
    )(x2d, g.reshape(1, -1), router)


def _routing_tables(ids, tm):
    tokens = ids.shape[0]
    n = TOP_K * tokens
    n_pad = N_EXPERTS * tm
    n_tiles = (n + n_pad) // tm
    e_flat = ids.T.reshape(n)
    experts = jnp.arange(N_EXPERTS, dtype=jnp.int32)
    counts = jnp.sum((e_flat[:, None] == experts[None, :]).astype(jnp.int32), axis=0)
    tiles_per = (counts + tm - 1) // tm
    tile_end = jnp.cumsum(tiles_per)
    n_used = tile_end[-1]
    jt = jnp.arange(n_tiles, dtype=jnp.int32)
    tile_expert = jnp.minimum(
        jnp.sum((jt[:, None] >= tile_end[None, :]).astype(jnp.int32), axis=1), N_EXPERTS - 1)
    pad_end = jnp.cumsum(tiles_per * tm - counts)
    pad_key = jnp.sum((jnp.arange(n_pad, dtype=jnp.int32)[:, None] >= pad_end[None, :])
                      .astype(jnp.int32), axis=1)
    row_dst = jnp.argsort(jnp.concatenate([e_flat, pad_key]), stable=True).astype(jnp.int32)
    row_token = jnp.where(row_dst < n, row_dst % tokens, 0).astype(jnp.int32)
    return tile_expert.astype(jnp.int32), n_used.reshape(1).astype(jnp.int32), row_token, row_dst


def _moe_group_kernel(te_ref, nu_ref, tok_ref, dst_ref,
                      ht_hbm, w1_hbm, w3_hbm, w2_hbm, y_hbm,
                      xbuf, h_buf, acc_buf, obuf, gather_sem, scatter_sem, *wbufs):
    weights = _ExpertWeights(w1_hbm, w3_hbm, w2_hbm, *wbufs)
    n_steps = weights.n_steps
    j = pl.program_id(0)
    f = pl.program_id(1)
    n_used = nu_ref[0]
    tm = h_buf.shape[0]
    chunk = tm // n_steps
    tail_rows = range(n_steps * chunk, tm)
    used = j < n_used
    has_prev = j > 0
    has_next = j + 1 < n_used
    slot = lax.rem(j, 2)

    def row_tile(r):
        return pl.ds(pl.multiple_of(r * ROW_TILES, ROW_TILES), ROW_TILES)

    def gather_row(tile, r, queue=0):
        pltpu.make_async_copy(ht_hbm.at[row_tile(tok_ref[tile * tm + r]), :],
                              xbuf.at[row_tile(r), :], gather_sem).start(priority=queue)

    def scatter_row(tile, r, queue=0):
        s = lax.rem(tile, 2)
        pltpu.make_async_copy(obuf.at[s].at[row_tile(r), :],
                              y_hbm.at[row_tile(dst_ref[tile * tm + r]), :],
                              scatter_sem.at[s]).start(priority=queue)

    def wait_gather():
        pltpu.make_async_copy(ht_hbm.at[pl.ds(0, tm * ROW_TILES), :], xbuf, gather_sem).wait()

    def wait_scatter(s):
        pltpu.make_async_copy(obuf.at[s], y_hbm.at[pl.ds(0, tm * ROW_TILES), :],
                              scatter_sem.at[s]).wait()

    def for_all_rows(row_fn, tile):
        def body(r, carry):
            row_fn(tile, r)
            return carry
        lax.fori_loop(0, tm, body, 0)

    @pl.when(jnp.logical_and(f == 0, j == 0))
    def _():
        for_all_rows(gather_row, 0)

    @pl.when(jnp.logical_and(f == 0, used))
    def _():
        wait_gather()
        for c, part in enumerate(_load_token_tiles(xbuf, tm)):
            h_buf[:, c * LANES:(c + 1) * LANES] = part.astype(BF16)
        acc_buf[...] = jnp.zeros_like(acc_buf)

    @pl.when(jnp.logical_and(f == 0, has_next))
    def _():
        for r in tail_rows:
            gather_row(j + 1, r)

    @pl.when(jnp.logical_and(f == 0, jnp.logical_and(used, has_prev)))
    def _():
        for r in tail_rows:
            scatter_row(j - 1, r)

    next_expert = te_ref[jnp.minimum(j + 1, pl.num_programs(0) - 1)]
    expert_ends = jnp.logical_and(has_next, next_expert != te_ref[j])

    @pl.when(j == 0)
    def _():
        weights.start_fetch(te_ref[0], f)
        weights.finish_fetch(f)

    @pl.when(expert_ends)
    def _():
        weights.start_fetch(next_expert, f)

    def step(do_gather, do_scatter):
        base = f * chunk
        for k in range(chunk):
            if do_gather:
                gather_row(j + 1, base + k, queue=k % 2)
            if do_scatter:
                scatter_row(j - 1, base + k, queue=k % 2)
        acc_buf[...] += weights.swiglu(h_buf[...], f)

    @pl.when(jnp.logical_and(has_prev, has_next))
    def _():
        step(True, True)

    @pl.when(jnp.logical_and(jnp.logical_not(has_prev), has_next))
    def _():
        step(True, False)

    @pl.when(jnp.logical_and(used, jnp.logical_and(has_prev, jnp.logical_not(has_next))))
    def _():
        step(False, True)

    @pl.when(jnp.logical_and(used, jnp.logical_not(jnp.logical_or(has_prev, has_next))))
    def _():
        step(False, False)

    @pl.when(expert_ends)
    def _():
        weights.finish_fetch(f)

    @pl.when(jnp.logical_and(f == n_steps - 1, used))
    def _():
        @pl.when(j >= 2)
        def _():
            wait_scatter(slot)
        _store_token_tiles(obuf.at[slot], acc_buf[...])

        @pl.when(jnp.logical_not(has_next))
        def _():
            for_all_rows(scatter_row, j)
            wait_scatter(slot)

            @pl.when(has_prev)
            def _():
                wait_scatter(1 - slot)

    @pl.when(jnp.logical_and(f == 0, jnp.logical_not(used)))
    def _():
        obuf[0] = jnp.zeros(obuf.shape[1:], F32)
        rows = tm * ROW_TILES
        fill = pltpu.make_async_copy(obuf.at[0],
                                     y_hbm.at[pl.ds(pl.multiple_of(j * rows, rows), rows), :],
                                     scatter_sem.at[0])
        fill.start()
        fill.wait()


def _moe_grouped(ht, w1, w3, w2, tables):
    tile_expert, n_used, row_token, row_dst = tables
    tm, fc = MOE_TM, FFN_FC
    n_rows = row_token.shape[0]
    tile_rows = tm * ROW_TILES
    n_steps = D_FF // fc
    hbm = pl.BlockSpec(memory_space=pl.ANY)
    grid_spec = pltpu.PrefetchScalarGridSpec(
        num_scalar_prefetch=4,
        grid=(n_rows // tm, n_steps),
        in_specs=[hbm, hbm, hbm, hbm],
        out_specs=hbm,
        scratch_shapes=[pltpu.VMEM((tile_rows, LANES), F32), pltpu.VMEM((tm, D_MODEL), BF16),
                        pltpu.VMEM((tm, D_MODEL), F32), pltpu.VMEM((2, tile_rows, LANES), F32),
                        pltpu.SemaphoreType.DMA(()), pltpu.SemaphoreType.DMA((2,))]
        + _expert_weight_scratch(n_steps, fc),
    )
    return pl.pallas_call(
        _moe_group_kernel,
        grid_spec=grid_spec,
        out_shape=jax.ShapeDtypeStruct((n_rows * ROW_TILES, LANES), F32),
        compiler_params=pltpu.CompilerParams(
            dimension_semantics=("arbitrary", "arbitrary"),
            vmem_limit_bytes=VMEM_LIMIT),
        name="moe_grouped",
    )(tile_expert, n_used, row_token, row_dst, ht, w1, w3, w2)


def _combine_kernel(x_ref, y0_ref, y1_ref, p_ref, gf_ref, out_ref, *, final_norm):
    tm = x_ref.shape[0]
    p = p_ref[...]
    y0 = _load_token_tiles(y0_ref, tm)
    y1 = _load_token_tiles(y1_ref, tm)
    y = jnp.concatenate(
        [x_ref[:, c * LANES:(c + 1) * LANES] + p[:, 0:1] * y0[c] + p[:, 1:2] * y1[c]
         for c in range(ROW_TILES)], axis=-1)
    out_ref[...] = _rms_norm(y, gf_ref[...]) if final_norm else y


def _combine(x2d, y_tiles, probs, g_final, *, final_norm):
    tokens = x2d.shape[0]
    tm = COMBINE_TM
    tile = pl.BlockSpec((tm, D_MODEL), lambda i: (i, 0))
    first = pl.BlockSpec((tm * ROW_TILES, LANES), lambda i: (i, 0))
    second = pl.BlockSpec((tm * ROW_TILES, LANES), lambda i: (i + tokens // tm, 0))
    return pl.pallas_call(
        functools.partial(_combine_kernel, final_norm=final_norm),
        grid=(tokens // tm,),
        in_specs=[tile, first, second,
                  pl.BlockSpec((tm, TOP_K), lambda i: (i, 0)),
                  pl.BlockSpec((1, D_MODEL), lambda i: (0, 0))],
        out_specs=tile,
        out_shape=jax.ShapeDtypeStruct(x2d.shape, F32),
        compiler_params=pltpu.CompilerParams(dimension_semantics=("arbitrary",)),
        name="moe_combine",
    )(x2d, y_tiles, y_tiles, probs, g_final.reshape(1, -1))


def _moe_ffn(x2d, g, g_final, router, w1, w3, w2, *, final_norm):
    ids, probs, ht = _router(x2d, g, router)
    tables = _routing_tables(ids, MOE_TM)
    y_tiles = _moe_grouped(ht, w1, w3, w2, tables)
    return _combine(x2d, y_tiles, probs, g_final, final_norm=final_norm)


def kernel(x, norm_mix_g, w_in, sc_conv_w, cf_conv_w, cf_conv_b, cf_ln_g, cf_ln_b,
           pool_w, pool_scale, w_sc_out, w_cf_out, w_pool_out, w_o, norm_ffn_g,
           dense_w1, dense_w3, dense_w2, moe_router, moe_w1, moe_w3, moe_w2,
           norm_final_g):
    batch, seq, d = x.shape
    depth = w_in.shape[0]
    for layer in range(depth):
        x = _mixer(x, norm_mix_g[layer], w_in[layer], sc_conv_w[layer], cf_conv_w[layer],
                   cf_conv_b[layer], cf_ln_g[layer], cf_ln_b[layer], pool_w[layer],
                   pool_scale[layer], w_sc_out[layer], w_cf_out[layer], w_pool_out[layer],
                   w_o[layer])
        x2d = x.reshape(batch * seq, d)
        i = layer // 2
        if layer % 2 == 0:
            x2d = _dense_ffn(x2d, norm_ffn_g[layer], dense_w1[i], dense_w3[i], dense_w2[i])
        else:
            x2d = _moe_ffn(x2d, norm_ffn_g[layer], norm_final_g, moe_router[i],
                           moe_w1[i], moe_w3[i], moe_w2[i],
                           final_norm=(layer == depth - 1))
        x = x2d.reshape(batch, seq, d)
    if depth % 2 == 1:
        x = _final_norm(x, norm_final_g)
    return x


def _final_norm_kernel(x_ref, g_ref, out_ref):
    out_ref[...] = _rms_norm(x_ref[...], g_ref[...])


def _final_norm(x, g):
    batch, seq, d = x.shape
    x2d = x.reshape(batch * seq, d)
    tile = pl.BlockSpec((FFN_TM, d), lambda i: (i, 0))
    out = pl.pallas_call(
        _final_norm_kernel,
        grid=(x2d.shape[0] // FFN_TM,),
        in_specs=[tile, pl.BlockSpec((1, d), lambda i: (0, 0))],
        out_specs=tile,
        out_shape=jax.ShapeDtypeStruct(x2d.shape, F32),
        name="final_norm",
    )(x2d, g.reshape(1, -1))
    return out.reshape(batch, seq, d)
```

```python
import functools

import jax
import jax.numpy as jnp
from jax import lax
from jax.experimental import pallas as pl
from jax.experimental.pallas import tpu as pltpu

D_MODEL = 1024
W_BRANCH = 1024
SC_KERNEL = 3
CF_KERNEL = 31
POOL_WINDOWS = (2, 4, 8, 16)
POOL_GROUP = W_BRANCH // len(POOL_WINDOWS)
D_FF = 3584
N_EXPERTS = 8
TOP_K = 2
RMS_EPS = 1e-6
LN_EPS = 1e-5

COL_SC = 0
COL_CF = 3 * W_BRANCH
COL_POOL = 5 * W_BRANCH
COL_GATE = 6 * W_BRANCH
IN_COLS = 9 * W_BRANCH

SUBLANES = 8
LANES = 128
ROW_TILES = D_MODEL // LANES
assert ROW_TILES == SUBLANES
assert POOL_WINDOWS == tuple(2 ** (g + 1) for g in range(len(POOL_WINDOWS)))

HALO_SC = SUBLANES
HALO_CF = SUBLANES * (-(-CF_KERNEL // SUBLANES))
HALO_POOL = SUBLANES * len(POOL_WINDOWS)

MIX_TM = 256
CONV_ROWS = 128
WEIGHT_COLS = 512
N_MIX_WEIGHTS = 8
FFN_TM = 512
FFN_FC = 512
MOE_TM = 512
ROUTE_TM = 512
COMBINE_TM = 512
VMEM_LIMIT = 60 * 1024 * 1024

F32 = jnp.float32
BF16 = jnp.bfloat16


def _rms_norm(x, g):
    ms = jnp.mean(x * x, axis=-1, keepdims=True)
    return x * lax.rsqrt(ms + RMS_EPS) * g


def _dot(a, b):
    return jnp.dot(a, b, preferred_element_type=F32)


def _dot_blocks(a, w_ref):
    return jnp.concatenate([_dot(a, w_ref[c]) for c in range(w_ref.shape[0])], axis=-1)


def _mixer_kernel(x_ref, g_ref, scw_ref, cfw_ref, cfb_ref, lng_ref, lnb_ref, poolw_ref,
                  pscale_ref, *rest):
    weights_hbm = rest[:N_MIX_WEIGHTS]
    out_ref, u_buf, v_buf, p_buf, conv_buf = rest[N_MIX_WEIGHTS:N_MIX_WEIGHTS + 5]
    weights_vmem = rest[N_MIX_WEIGHTS + 5:2 * N_MIX_WEIGHTS + 5]
    weight_sem = rest[2 * N_MIX_WEIGHTS + 5]
    w_sc_ref, w_cf_ref, w_pool_ref, w_gate_ref, wa_ref, wb_ref, wc_ref, wo_ref = weights_vmem
    i = pl.program_id(1)
    tm = x_ref.shape[0]

    @pl.when(jnp.logical_and(pl.program_id(0) == 0, i == 0))
    def _():
        copies = []
        for n, (hbm, vmem) in enumerate(zip(weights_hbm, weights_vmem)):
            for c in range(vmem.shape[0]):
                copies.append(pltpu.make_async_copy(
                    hbm.at[:, c * WEIGHT_COLS:(c + 1) * WEIGHT_COLS], vmem.at[c], weight_sem.at[n]))
        for copy in copies:
            copy.start()
        for copy in copies:
            copy.wait()

    @pl.when(i == 0)
    def _():
        u_buf[0:HALO_SC, :] = jnp.zeros((HALO_SC, W_BRANCH), F32)
        v_buf[0:HALO_CF, :] = jnp.zeros((HALO_CF, W_BRANCH), F32)
        p_buf[0:HALO_POOL, :] = jnp.zeros((HALO_POOL, W_BRANCH), F32)

    x = x_ref[...]
    h = _rms_norm(x, g_ref[...]).astype(BF16)

    pb = _dot_blocks(h, w_cf_ref)
    v = pb[:, 0:W_BRANCH] * jax.nn.sigmoid(pb[:, W_BRANCH:2 * W_BRANCH])
    v_buf[HALO_CF:HALO_CF + tm, :] = v
    pa = _dot_blocks(h, w_sc_ref)
    pu = _dot_blocks(h, w_pool_ref)
    gate_logits = _dot_blocks(h, w_gate_ref)

    ext = CONV_ROWS + SUBLANES
    for row0 in range(0, tm, CONV_ROWS):
        for c in range(W_BRANCH // LANES):
            lanes = slice(c * LANES, (c + 1) * LANES)
            total = None
            for r in range(SUBLANES):
                part = None
                for q in range(-(-CF_KERNEL // SUBLANES)):
                    d = SUBLANES * q + r
                    if d < CF_KERNEL:
                        base = row0 + HALO_CF - SUBLANES * (q + 1)
                        term = (cfw_ref[CF_KERNEL - 1 - d:CF_KERNEL - d, lanes]
                                * v_buf[base:base + ext, lanes])
                        part = term if part is None else part + term
                part = part[SUBLANES - r:SUBLANES - r + CONV_ROWS, :]
                total = part if total is None else total + part
            conv_buf[row0:row0 + CONV_ROWS, lanes] = total + cfb_ref[:, lanes]
    acc = conv_buf[...]
    mu = jnp.mean(acc, axis=-1, keepdims=True)
    xc = acc - mu
    var = jnp.mean(xc * xc, axis=-1, keepdims=True)
    ln = xc * lax.rsqrt(var + LN_EPS) * lng_ref[...] + lnb_ref[...]
    y_b = _dot_blocks((ln * jax.nn.sigmoid(ln)).astype(BF16), wb_ref)
    v_buf[0:HALO_CF, :] = v_buf[tm:tm + HALO_CF, :]

    u = pa[:, W_BRANCH:2 * W_BRANCH] * pa[:, 2 * W_BRANCH:3 * W_BRANCH]
    u_buf[HALO_SC:HALO_SC + tm, :] = u
    conv = scw_ref[SC_KERNEL - 1:SC_KERNEL, :] * u
    for k in range(SC_KERNEL - 1):
        off = HALO_SC - (SC_KERNEL - 1) + k
        conv = conv + scw_ref[k:k + 1, :] * u_buf[off:off + tm, :]
    y_a = _dot_blocks((pa[:, 0:W_BRANCH] * conv).astype(BF16), wa_ref)
    u_buf[0:HALO_SC, :] = u_buf[tm:tm + HALO_SC, :]

    p_buf[HALO_POOL:HALO_POOL + tm, :] = pu
    pos = (i * tm + lax.broadcasted_iota(jnp.int32, (tm, 1), 0)).astype(F32)
    win_sums = {}
    level, w = p_buf[...], 1
    for gi in range(len(POOL_WINDOWS)):
        level = level[SUBLANES:, :] + level[SUBLANES - w:level.shape[0] - w, :]
        w *= 2
        first_row = HALO_POOL - SUBLANES * (gi + 1)
        win_sums[w] = level[first_row:first_row + tm, 0:POOL_GROUP]
        if gi + 1 < len(POOL_WINDOWS):
            level = level[:, POOL_GROUP:]
    q_groups = []
    for gi, w in enumerate(POOL_WINDOWS):
        tok = pu[:, gi * POOL_GROUP:(gi + 1) * POOL_GROUP]
        inv_count = 1.0 / jnp.minimum(pos + 1.0, float(w))
        q_groups.append(_dot((win_sums[w] * inv_count - tok).astype(BF16), poolw_ref[gi]))
    q = jnp.concatenate(q_groups, axis=-1) * pscale_ref[...]
    y_c = _dot_blocks(q.astype(BF16), wc_ref)
    p_buf[0:HALO_POOL, :] = p_buf[tm:tm + HALO_POOL, :]

    gates = jax.nn.sigmoid(gate_logits)
    merged = (gates[:, 0:D_MODEL] * y_a + gates[:, D_MODEL:2 * D_MODEL] * y_b
              + gates[:, 2 * D_MODEL:3 * D_MODEL] * y_c)
    out_ref[...] = x + _dot_blocks(merged.astype(BF16), wo_ref)


def _resident(shape):
    zeros = (0,) * len(shape)
    return pl.BlockSpec(shape, lambda b, i: zeros, pipeline_mode=pl.Buffered(1))


def _mixer(x, g, w_in, scw, cfw, cfb, lng, lnb, poolw, pscale, wa, wb, wc, wo):
    batch, seq, _ = x.shape
    tm = MIX_TM
    row = lambda v: v.reshape(1, -1)
    tile = pl.BlockSpec((None, tm, D_MODEL), lambda b, i: (b, i, 0))
    w_groups = [w_in[:, lo:hi] for lo, hi in
                ((COL_SC, COL_CF), (COL_CF, COL_POOL), (COL_POOL, COL_GATE), (COL_GATE, IN_COLS))]
    weights = [w.astype(BF16) for w in w_groups + [wa, wb, wc, wo]]
    assert len(weights) == N_MIX_WEIGHTS
    small = (row(g), scw, cfw, row(cfb), row(lng), row(lnb), poolw.astype(BF16), row(pscale))
    return pl.pallas_call(
        _mixer_kernel,
        grid=(batch, seq // tm),
        in_specs=[tile] + [_resident(op.shape) for op in small]
        + [pl.BlockSpec(memory_space=pl.ANY)] * N_MIX_WEIGHTS,
        out_specs=tile,
        out_shape=jax.ShapeDtypeStruct(x.shape, F32),
        scratch_shapes=[pltpu.VMEM((HALO_SC + tm, W_BRANCH), F32),
                        pltpu.VMEM((HALO_CF + tm, W_BRANCH), F32),
                        pltpu.VMEM((HALO_POOL + tm, W_BRANCH), F32),
                        pltpu.VMEM((tm, W_BRANCH), F32)]
        + [pltpu.VMEM((w.shape[1] // WEIGHT_COLS, w.shape[0], WEIGHT_COLS), BF16) for w in weights]
        + [pltpu.SemaphoreType.DMA((N_MIX_WEIGHTS,))],
        compiler_params=pltpu.CompilerParams(
            dimension_semantics=("arbitrary", "arbitrary"),
            vmem_limit_bytes=VMEM_LIMIT),
        name="mixer",
    )(x, *small, *weights)


class _ExpertWeights:
    def __init__(self, w1_hbm, w3_hbm, w2_hbm, w1r, w3r, w2r, stage13, stage2, sem):
        self.hbm = (w1_hbm, w3_hbm, w2_hbm)
        self.w1r, self.w3r, self.w2r = w1r, w3r, w2r
        self.stage13, self.stage2, self.sem = stage13, stage2, sem
        self.n_steps, _, self.fc = w1r.shape

    def _copies(self, e, k):
        w1_hbm, w3_hbm, w2_hbm = self.hbm
        cols = slice(k * self.fc, (k + 1) * self.fc)
        return (pltpu.make_async_copy(w1_hbm.at[e, :, cols], self.stage13.at[0], self.sem.at[0]),
                pltpu.make_async_copy(w3_hbm.at[e, :, cols], self.stage13.at[1], self.sem.at[1]),
                pltpu.make_async_copy(w2_hbm.at[e, cols, :], self.stage2, self.sem.at[2]))

    def start_fetch(self, e, f):
        if isinstance(f, int):
            for copy in self._copies(e, f):
                copy.start()
            return
        for k in range(self.n_steps):
            @pl.when(f == k)
            def _():
                for copy in self._copies(e, k):
                    copy.start()

    def finish_fetch(self, f):
        for copy in self._copies(0, 0):
            copy.wait()
        self.w1r[f] = self.stage13[0].astype(BF16)
        self.w3r[f] = self.stage13[1].astype(BF16)
        self.w2r[f] = self.stage2[...].astype(BF16)

    def swiglu(self, h, f):
        a = _dot(h, self.w1r[f])
        c = _dot(h, self.w3r[f])
        return _dot((a * jax.nn.sigmoid(a) * c).astype(BF16), self.w2r[f])


def _expert_weight_scratch(n_steps, fc):
    return [pltpu.VMEM((n_steps, D_MODEL, fc), BF16), pltpu.VMEM((n_steps, D_MODEL, fc), BF16),
            pltpu.VMEM((n_steps, fc, D_MODEL), BF16), pltpu.VMEM((2, D_MODEL, fc), F32),
            pltpu.VMEM((fc, D_MODEL), F32), pltpu.SemaphoreType.DMA((3,))]


def _dense_ffn_kernel(x_ref, g_ref, w1_hbm, w3_hbm, w2_hbm, out_ref, h_buf, *wbufs):
    weights = _ExpertWeights(w1_hbm, w3_hbm, w2_hbm, *wbufs)

    @pl.when(pl.program_id(0) == 0)
    def _():
        for k in range(weights.n_steps):
            weights.start_fetch(0, k)
            weights.finish_fetch(k)

    x = x_ref[...]
    h_buf[...] = _rms_norm(x, g_ref[...]).astype(BF16)
    acc = x
    for k in range(weights.n_steps):
        acc = acc + weights.swiglu(h_buf[...], k)
    out_ref[...] = acc


def _dense_ffn(x2d, g, w1, w3, w2):
    tokens = x2d.shape[0]
    tm, fc = FFN_TM, FFN_FC
    n_steps = D_FF // fc
    tile = pl.BlockSpec((tm, D_MODEL), lambda i: (i, 0))
    hbm = pl.BlockSpec(memory_space=pl.ANY)
    return pl.pallas_call(
        _dense_ffn_kernel,
        grid=(tokens // tm,),
        in_specs=[tile, pl.BlockSpec((1, D_MODEL), lambda i: (0, 0)), hbm, hbm, hbm],
        out_specs=tile,
        out_shape=jax.ShapeDtypeStruct(x2d.shape, F32),
        scratch_shapes=[pltpu.VMEM((tm, D_MODEL), BF16)] + _expert_weight_scratch(n_steps, fc),
        compiler_params=pltpu.CompilerParams(
            dimension_semantics=("arbitrary",),
            vmem_limit_bytes=VMEM_LIMIT),
        name="dense_ffn",
    )(x2d, g.reshape(1, -1), w1[None], w3[None], w2[None])


def _store_token_tiles(ref, value):
    rows = value.shape[0]
    for c in range(ROW_TILES):
        ref[pl.ds(c, rows, stride=ROW_TILES), :] = value[:, c * LANES:(c + 1) * LANES]


def _load_token_tiles(ref, rows):
    return [ref[pl.ds(c, rows, stride=ROW_TILES), :] for c in range(ROW_TILES)]


def _router_kernel(x_ref, g_ref, router_ref, ids_ref, probs_ref, ht_ref):
    h = _rms_norm(x_ref[...], g_ref[...])
    _store_token_tiles(ht_ref, h)
    lane = lax.broadcasted_iota(jnp.int32, (h.shape[0], N_EXPERTS), 1)
    logits = jnp.zeros(lane.shape, F32)
    for e in range(N_EXPERTS):
        col = jnp.sum(h * router_ref[e:e + 1, :], axis=-1, keepdims=True)
        logits = jnp.where(lane == e, col, logits)
    m1 = jnp.max(logits, axis=-1, keepdims=True)
    i1 = jnp.min(jnp.where(logits == m1, lane, N_EXPERTS), axis=-1, keepdims=True)
    rest = jnp.where(lane == i1, -jnp.inf, logits)
    m2 = jnp.max(rest, axis=-1, keepdims=True)
    i2 = jnp.min(jnp.where(rest == m2, lane, N_EXPERTS), axis=-1, keepdims=True)
    e2 = jnp.exp(m2 - m1)
    denom = 1.0 + e2
    first = lax.broadcasted_iota(jnp.int32, ids_ref.shape, 1) == 0
    ids_ref[...] = jnp.where(first, i1, i2)
    probs_ref[...] = jnp.where(first, 1.0 / denom, e2 / denom)


def _router(x2d, g, router):
    tokens = x2d.shape[0]
    tm = ROUTE_TM
    pair = pl.BlockSpec((tm, TOP_K), lambda i: (i, 0))
    return pl.pallas_call(
        _router_kernel,
        grid=(tokens // tm,),
        in_specs=[pl.BlockSpec((tm, D_MODEL), lambda i: (i, 0)),
                  pl.BlockSpec((1, D_MODEL), lambda i: (0, 0)),
                  pl.BlockSpec((N_EXPERTS, D_MODEL), lambda i: (0, 0))],
        out_specs=[pair, pair, pl.BlockSpec((tm * ROW_TILES, LANES), lambda i: (i, 0))],
        out_shape=[jax.ShapeDtypeStruct((tokens, TOP_K), jnp.int32),
                   jax.ShapeDtypeStruct((tokens, TOP_K), F32),
                   jax.ShapeDtypeStruct((tokens * ROW_TILES, LANES), F32)],
        compiler_params=pltpu.CompilerParams(dimension_semantics=("arbitrary",)),
        name="router",
    )(x2d, g.reshape(1, -1), router.T)


def _routing_tables(ids, tm):
    tokens = ids.shape[0]
    n = TOP_K * tokens
    n_pad = N_EXPERTS * tm
    n_tiles = (n + n_pad) // tm
    e_flat = ids.T.reshape(n)
    experts = jnp.arange(N_EXPERTS, dtype=jnp.int32)
    counts = jnp.sum((e_flat[:, None] == experts[None, :]).astype(jnp.int32), axis=0)
    tiles_per = (counts + tm - 1) // tm
    tile_end = jnp.cumsum(tiles_per)
    n_used = tile_end[-1]
    jt = jnp.arange(n_tiles, dtype=jnp.int32)
    tile_expert = jnp.minimum(
        jnp.sum((jt[:, None] >= tile_end[None, :]).astype(jnp.int32), axis=1), N_EXPERTS - 1)
    pad_end = jnp.cumsum(tiles_per * tm - counts)
    pad_key = jnp.sum((jnp.arange(n_pad, dtype=jnp.int32)[:, None] >= pad_end[None, :])
                      .astype(jnp.int32), axis=1)
    row_dst = jnp.argsort(jnp.concatenate([e_flat, pad_key]), stable=True).astype(jnp.int32)
    row_token = jnp.where(row_dst < n, row_dst % tokens, 0).astype(jnp.int32)
    return tile_expert.astype(jnp.int32), n_used.reshape(1).astype(jnp.int32), row_token, row_dst


def _moe_group_kernel(te_ref, nu_ref, tok_ref, dst_ref,
                      ht_hbm, w1_hbm, w3_hbm, w2_hbm, y_hbm,
                      xbuf, h_buf, acc_buf, obuf, gather_sem, scatter_sem, *wbufs):
    weights = _ExpertWeights(w1_hbm, w3_hbm, w2_hbm, *wbufs)
    n_steps = weights.n_steps
    j = pl.program_id(0)
    f = pl.program_id(1)
    n_used = nu_ref[0]
    tm = h_buf.shape[0]
    chunk = tm // n_steps
    tail_rows = range(n_steps * chunk, tm)
    used = j < n_used
    has_prev = j > 0
    has_next = j + 1 < n_used
    slot = lax.rem(j, 2)

    def row_tile(r):
        return pl.ds(pl.multiple_of(r * ROW_TILES, ROW_TILES), ROW_TILES)

    def gather_row(tile, r, queue=0):
        pltpu.make_async_copy(ht_hbm.at[row_tile(tok_ref[tile * tm + r]), :],
                              xbuf.at[row_tile(r), :], gather_sem).start(priority=queue)

    def scatter_row(tile, r, queue=0):
        s = lax.rem(tile, 2)
        pltpu.make_async_copy(obuf.at[s].at[row_tile(r), :],
                              y_hbm.at[row_tile(dst_ref[tile * tm + r]), :],
                              scatter_sem.at[s]).start(priority=queue)

    def wait_gather():
        pltpu.make_async_copy(ht_hbm.at[pl.ds(0, tm * ROW_TILES), :], xbuf, gather_sem).wait()

    def wait_scatter(s):
        pltpu.make_async_copy(obuf.at[s], y_hbm.at[pl.ds(0, tm * ROW_TILES), :],
                              scatter_sem.at[s]).wait()

    def for_all_rows(row_fn, tile):
        def body(r, carry):
            row_fn(tile, r)
            return carry
        lax.fori_loop(0, tm, body, 0)

    @pl.when(jnp.logical_and(f == 0, j == 0))
    def _():
        for_all_rows(gather_row, 0)

    @pl.when(jnp.logical_and(f == 0, used))
    def _():
        wait_gather()
        for c, part in enumerate(_load_token_tiles(xbuf, tm)):
            h_buf[:, c * LANES:(c + 1) * LANES] = part.astype(BF16)
        acc_buf[...] = jnp.zeros_like(acc_buf)

    @pl.when(jnp.logical_and(f == 0, has_next))
    def _():
        for r in tail_rows:
            gather_row(j + 1, r)

    @pl.when(jnp.logical_and(f == 0, jnp.logical_and(used, has_prev)))
    def _():
        for r in tail_rows:
            scatter_row(j - 1, r)

    next_expert = te_ref[jnp.minimum(j + 1, pl.num_programs(0) - 1)]
    expert_ends = jnp.logical_and(has_next, next_expert != te_ref[j])

    @pl.when(j == 0)
    def _():
        weights.start_fetch(te_ref[0], f)
        weights.finish_fetch(f)

    @pl.when(expert_ends)
    def _():
        weights.start_fetch(next_expert, f)

    def step(do_gather, do_scatter):
        base = f * chunk
        for k in range(chunk):
            if do_gather:
                gather_row(j + 1, base + k, queue=k % 2)
            if do_scatter:
                scatter_row(j - 1, base + k, queue=k % 2)
        acc_buf[...] += weights.swiglu(h_buf[...], f)

    @pl.when(jnp.logical_and(has_prev, has_next))
    def _():
        step(True, True)

    @pl.when(jnp.logical_and(jnp.logical_not(has_prev), has_next))
    def _():
        step(True, False)

    @pl.when(jnp.logical_and(used, jnp.logical_and(has_prev, jnp.logical_not(has_next))))
    def _():
        step(False, True)

    @pl.when(jnp.logical_and(used, jnp.logical_not(jnp.logical_or(has_prev, has_next))))
    def _():
        step(False, False)

    @pl.when(expert_ends)
    def _():
        weights.finish_fetch(f)

    @pl.when(jnp.logical_and(f == n_steps - 1, used))
    def _():
        @pl.when(j >= 2)
        def _():
            wait_scatter(slot)
        _store_token_tiles(obuf.at[slot], acc_buf[...])

        @pl.when(jnp.logical_not(has_next))
        def _():
            for_all_rows(scatter_row, j)
            wait_scatter(slot)

            @pl.when(has_prev)
            def _():
                wait_scatter(1 - slot)

    @pl.when(jnp.logical_and(f == 0, jnp.logical_not(used)))
    def _():
        obuf[0] = jnp.zeros(obuf.shape[1:], F32)
        rows = tm * ROW_TILES
        fill = pltpu.make_async_copy(obuf.at[0],
                                     y_hbm.at[pl.ds(pl.multiple_of(j * rows, rows), rows), :],
                                     scatter_sem.at[0])
        fill.start()
        fill.wait()


def _moe_grouped(ht, w1, w3, w2, tables):
    tile_expert, n_used, row_token, row_dst = tables
    tm, fc = MOE_TM, FFN_FC
    n_rows = row_token.shape[0]
    tile_rows = tm * ROW_TILES
    n_steps = D_FF // fc
    hbm = pl.BlockSpec(memory_space=pl.ANY)
    grid_spec = pltpu.PrefetchScalarGridSpec(
        num_scalar_prefetch=4,
        grid=(n_rows // tm, n_steps),
        in_specs=[hbm, hbm, hbm, hbm],
        out_specs=hbm,
        scratch_shapes=[pltpu.VMEM((tile_rows, LANES), F32), pltpu.VMEM((tm, D_MODEL), BF16),
                        pltpu.VMEM((tm, D_MODEL), F32), pltpu.VMEM((2, tile_rows, LANES), F32),
                        pltpu.SemaphoreType.DMA(()), pltpu.SemaphoreType.DMA((2,))]
        + _expert_weight_scratch(n_steps, fc),
    )
    return pl.pallas_call(
        _moe_group_kernel,
        grid_spec=grid_spec,
        out_shape=jax.ShapeDtypeStruct((n_rows * ROW_TILES, LANES), F32),
        compiler_params=pltpu.CompilerParams(
            dimension_semantics=("arbitrary", "arbitrary"),
            vmem_limit_bytes=VMEM_LIMIT),
        name="moe_grouped",
    )(tile_expert, n_used, row_token, row_dst, ht, w1, w3, w2)


def _combine_kernel(x_ref, y0_ref, y1_ref, p_ref, gf_ref, out_ref, *, final_norm):
    tm = x_ref.shape[0]
    p = p_ref[...]
    y0 = _load_token_tiles(y0_ref, tm)
    y1 = _load_token_tiles(y1_ref, tm)
    y = jnp.concatenate(
        [x_ref[:, c * LANES:(c + 1) * LANES] + p[:, 0:1] * y0[c] + p[:, 1:2] * y1[c]
         for c in range(ROW_TILES)], axis=-1)
    out_ref[...] = _rms_norm(y, gf_ref[...]) if final_norm else y


def _combine(x2d, y_tiles, probs, g_final, *, final_norm):
    tokens = x2d.shape[0]
    tm = COMBINE_TM
    tile = pl.BlockSpec((tm, D_MODEL), lambda i: (i, 0))
    first = pl.BlockSpec((tm * ROW_TILES, LANES), lambda i: (i, 0))
    second = pl.BlockSpec((tm * ROW_TILES, LANES), lambda i: (i + tokens // tm, 0))
    return pl.pallas_call(
        functools.partial(_combine_kernel, final_norm=final_norm),
        grid=(tokens // tm,),
        in_specs=[tile, first, second,
                  pl.BlockSpec((tm, TOP_K), lambda i: (i, 0)),
                  pl.BlockSpec((1, D_MODEL), lambda i: (0, 0))],
        out_specs=tile,
        out_shape=jax.ShapeDtypeStruct(x2d.shape, F32),
        compiler_params=pltpu.CompilerParams(dimension_semantics=("arbitrary",)),
        name="moe_combine",
    )(x2d, y_tiles, y_tiles, probs, g_final.reshape(1, -1))


def _moe_ffn(x2d, g, g_final, router, w1, w3, w2, *, final_norm):
    ids, probs, ht = _router(x2d, g, router)
    tables = _routing_tables(ids, MOE_TM)
    y_tiles = _moe_grouped(ht, w1, w3, w2, tables)
    return _combine(x2d, y_tiles, probs, g_final, final_norm=final_norm)


def kernel(x, norm_mix_g, w_in, sc_conv_w, cf_conv_w, cf_conv_b, cf_ln_g, cf_ln_b,
           pool_w, pool_scale, w_sc_out, w_cf_out, w_pool_out, w_o, norm_ffn_g,
           dense_w1, dense_w3, dense_w2, moe_router, moe_w1, moe_w3, moe_w2,
           norm_final_g):
    batch, seq, d = x.shape
    depth = w_in.shape[0]
    for layer in range(depth):
        x = _mixer(x, norm_mix_g[layer], w_in[layer], sc_conv_w[layer], cf_conv_w[layer],
                   cf_conv_b[layer], cf_ln_g[layer], cf_ln_b[layer], pool_w[layer],
                   pool_scale[layer], w_sc_out[layer], w_cf_out[layer], w_pool_out[layer],
                   w_o[layer])
        x2d = x.reshape(batch * seq, d)
        i = layer // 2
        if layer % 2 == 0:
            x2d = _dense_ffn(x2d, norm_ffn_g[layer], dense_w1[i], dense_w3[i], dense_w2[i])
        else:
            x2d = _moe_ffn(x2d, norm_ffn_g[layer], norm_final_g, moe_router[i],
                           moe_w1[i], moe_w3[i], moe_w2[i],
                           final_norm=(layer == depth - 1))
        x = x2d.reshape(batch, seq, d)
    if depth % 2 == 1:
        x = _final_norm(x, norm_final_g)
    return x


def _final_norm_kernel(x_ref, g_ref, out_ref):
    out_ref[...] = _rms_norm(x_ref[...], g_ref[...])


def _final_norm(x, g):
    batch, seq, d = x.shape
    x2d = x.reshape(batch * seq, d)
    tile = pl.BlockSpec((FFN_TM, d), lambda i: (i, 0))
    out = pl.pallas_call(
        _final_norm_kernel,
        grid=(x2d.shape[0] // FFN_TM,),
        in_specs=[tile, pl.BlockSpec((1, d), lambda i: (0, 0))],
        out_specs=tile,
        out_shape=jax.ShapeDtypeStruct(x2d.shape, F32),
        name="final_norm",
    )(x2d, g.reshape(1, -1))
    return out.reshape(batch, seq, d)
```
